```python
import math
import jax, jax.numpy as jnp
from jax import lax
import numpy as np

D_MODEL = 1024
BATCH = 4
SEQ = 8192
DEPTH = 4

CTX_LEN = 256
GRID_W = 64
N_MIXERS = 3
MIXER_POOL = 0
MIXER_ATTN = 1
MIXER_RET = 2
POOL_WINDOWS = (2, 4, 8, 16)
POOL_GROUP = D_MODEL // 4
ATTN_HEADS = 8
ATTN_KV_HEADS = 2
ATTN_HEAD_DIM = D_MODEL // ATTN_HEADS
ATTN_GROUP = ATTN_HEADS // ATTN_KV_HEADS
Q_BLOCK = 128
ROPE_THETA = 10000.0
RET_HEADS = 4
RET_DK = D_MODEL // RET_HEADS
RET_DV = 2 * D_MODEL // RET_HEADS
RET_CHUNK = 128
D_FF = 2816
CONV_WIDTH = 3
EPS = 1e-6

kernel_name = 'hybrid_pool_gqa_retention_convffn_dit'


def _rmsnorm(x, gain):
    xf = x.astype(jnp.float32)
    y = xf * lax.rsqrt(jnp.mean(xf * xf, axis=-1, keepdims=True) + EPS)
    return (y * gain.astype(jnp.float32)).astype(x.dtype)


def _modulate(h, shift, scale):
    return h * (1.0 + scale) + shift


def _pool_mixer(h, w, b, scale):
    s = h.shape[1]
    hf = h.astype(jnp.float32)
    cs = jnp.concatenate([jnp.zeros_like(hf[:, :1]), jnp.cumsum(hf, axis=1)], axis=1)
    t = jnp.arange(s)
    outs = []
    for g, win in enumerate(POOL_WINDOWS):
        lo = jnp.clip(t - win // 2, 0, s)
        hi = jnp.clip(t + win // 2, 0, s)
        sl = slice(g * POOL_GROUP, (g + 1) * POOL_GROUP)
        csg = cs[..., sl]
        mean = (csg[:, hi] - csg[:, lo]) / (hi - lo).astype(jnp.float32)[None, :, None]
        outs.append((mean - hf[..., sl]).astype(h.dtype) @ w[g])
    return (jnp.concatenate(outs, axis=-1) + b) * scale


def _axial_angles(n_tokens):
    rows = n_tokens // GRID_W
    row = jnp.broadcast_to(jnp.arange(rows)[:, None], (rows, GRID_W)).reshape(-1).astype(jnp.float32)
    col = jnp.broadcast_to(jnp.arange(GRID_W)[None, :], (rows, GRID_W)).reshape(-1).astype(jnp.float32)
    axis_dim = ATTN_HEAD_DIM // 2
    inv = ROPE_THETA ** (-jnp.arange(0, axis_dim, 2, dtype=jnp.float32) / axis_dim)
    return row[:, None] * inv, col[:, None] * inv


def _rotate(xa, ang):
    x1, x2 = jnp.split(xa, 2, axis=-1)
    cos = jnp.cos(ang)[None, :, None, :]
    sin = jnp.sin(ang)[None, :, None, :]
    return jnp.concatenate([x1 * cos - x2 * sin, x1 * sin + x2 * cos], axis=-1)


def _apply_axial_rope(x, ang_row, ang_col):
    xr, xc = jnp.split(x.astype(jnp.float32), 2, axis=-1)
    return jnp.concatenate([_rotate(xr, ang_row), _rotate(xc, ang_col)], axis=-1).astype(x.dtype)


def _gqa(q, k, v):
    s = jnp.einsum('bqkgd,bskd->bkgqs', q, k).astype(jnp.float32) * (ATTN_HEAD_DIM ** -0.5)
    p = jax.nn.softmax(s, axis=-1).astype(v.dtype)
    return jnp.einsum('bkgqs,bskd->bqkgd', p, v)


def _attention_mixer(h_lat, h_ctx, w_qkv, q_gain, k_gain, w_o, need_ctx_out):
    def project(h):
        b, s, _ = h.shape
        nq = ATTN_HEADS * ATTN_HEAD_DIM
        nk = ATTN_KV_HEADS * ATTN_HEAD_DIM
        q, k, v = jnp.split(h @ w_qkv, [nq, nq + nk], axis=-1)
        q = _rmsnorm(q.reshape(b, s, ATTN_HEADS, ATTN_HEAD_DIM), q_gain)
        k = _rmsnorm(k.reshape(b, s, ATTN_KV_HEADS, ATTN_HEAD_DIM), k_gain)
        return q, k, v.reshape(b, s, ATTN_KV_HEADS, ATTN_HEAD_DIM)

    b, s = h_lat.shape[:2]
    qc, kc, vc = project(h_ctx)
    ql, kl, vl = project(h_lat)
    ang_r, ang_c = _axial_angles(s)
    ql = _apply_axial_rope(ql, ang_r, ang_c)
    kl = _apply_axial_rope(kl, ang_r, ang_c)
    keys = jnp.concatenate([kl, kc], axis=1)
    vals = jnp.concatenate([vl, vc], axis=1)
    q_blocks = ql.reshape(b, s // Q_BLOCK, Q_BLOCK, ATTN_KV_HEADS, ATTN_GROUP, ATTN_HEAD_DIM)
    q_blocks = q_blocks.transpose(1, 0, 2, 3, 4, 5)
    o = lax.map(lambda qb: _gqa(qb, keys, vals), q_blocks)
    o = o.transpose(1, 0, 2, 3, 4, 5).reshape(b, s, ATTN_HEADS * ATTN_HEAD_DIM)
    y_lat = o @ w_o
    y_ctx = None
    if need_ctx_out:
        l = h_ctx.shape[1]
        oc = _gqa(qc.reshape(b, l, ATTN_KV_HEADS, ATTN_GROUP, ATTN_HEAD_DIM), kc, vc)
        y_ctx = oc.reshape(b, l, ATTN_HEADS * ATTN_HEAD_DIM) @ w_o
    return y_lat, y_ctx


def _retention_chunks(q, k, v, log_gamma, state):
    b, h, s, _ = q.shape
    dv = v.shape[-1]
    n = s // RET_CHUNK

    def chunks(a):
        return jnp.moveaxis(a.reshape(b, h, n, RET_CHUNK, a.shape[-1]), 2, 0)

    idx = jnp.arange(RET_CHUNK, dtype=jnp.float32)
    diff = idx[:, None] - idx[None, :]
    intra = jnp.where(diff >= 0, jnp.exp(jnp.maximum(diff, 0.0) * log_gamma[:, None, None]), 0.0)
    q_dec = jnp.exp((idx + 1.0) * log_gamma[:, None])[None, :, :, None]
    k_dec = jnp.exp((RET_CHUNK - 1.0 - idx) * log_gamma[:, None])[None, :, :, None]
    chunk_dec = jnp.exp(RET_CHUNK * log_gamma)[None, :, None, None]

    def step(r, qkv):
        qc, kc, vc = qkv
        att = jnp.einsum('bhid,bhjd->bhij', qc, kc) * intra
        o = jnp.einsum('bhij,bhje->bhie', att, vc) + jnp.einsum('bhid,bhde->bhie', qc, r) * q_dec
        r = r * chunk_dec + jnp.einsum('bhjd,bhje->bhde', kc * k_dec, vc)
        return r, o

    r, o = lax.scan(step, state, (chunks(q), chunks(k), chunks(v)))
    return jnp.moveaxis(o, 0, 2).reshape(b, h, s, dv), r


def _final_state(k, v, log_gamma):
    l = k.shape[2]
    w = jnp.exp((l - 1.0 - jnp.arange(l, dtype=jnp.float32)) * log_gamma[:, None])
    return jnp.einsum('bhjd,bhje->bhde', k * w[None, :, :, None], v)


def _retention_mixer(h_lat, h_ctx, w_in, decay_logit, gn_w, w_out, need_ctx_out):
    log_gamma = jax.nn.log_sigmoid(decay_logit.astype(jnp.float32))
    nq = RET_HEADS * RET_DK
    nv = RET_HEADS * RET_DV

    def project(h):
        b, s, _ = h.shape
        q, k, v, g = jnp.split(h @ w_in, [nq, 2 * nq, 2 * nq + nv], axis=-1)

        def heads(a, d):
            return a.reshape(b, s, RET_HEADS, d).transpose(0, 2, 1, 3).astype(jnp.float32)
        return heads(q, RET_DK), heads(k, RET_DK) * (RET_DK ** -0.5), heads(v, RET_DV), g

    def flip(a):
        return jnp.flip(a, axis=2)

    def readout(y, g):
        b, h, s, dv = y.shape
        mu = jnp.mean(y, axis=-1, keepdims=True)
        var = jnp.mean(jnp.square(y - mu), axis=-1, keepdims=True)
        yn = (y - mu) * lax.rsqrt(var + EPS) * gn_w.astype(jnp.float32).reshape(1, h, 1, dv)
        yn = yn.transpose(0, 2, 1, 3).reshape(b, s, h * dv).astype(g.dtype)
        return (jax.nn.silu(g) * yn) @ w_out

    y_ctx = None
    if need_ctx_out:
        qc, kc, vc, gc = project(h_ctx)
        zero = jnp.zeros(qc.shape[:2] + (RET_DK, RET_DV), jnp.float32)
        oc_f, r_f = _retention_chunks(qc, kc, vc, log_gamma[0], zero)
        oc_b, r_b = _retention_chunks(flip(qc), flip(kc), flip(vc), log_gamma[1], zero)
        y_ctx = readout(oc_f + flip(oc_b), gc)
    else:
        b, l, _ = h_ctx.shape
        kv_c = (h_ctx @ w_in[:, nq:2 * nq + nv])
        kc = kv_c[..., :nq].reshape(b, l, RET_HEADS, RET_DK).transpose(0, 2, 1, 3).astype(jnp.float32) * (RET_DK ** -0.5)
        vc = kv_c[..., nq:].reshape(b, l, RET_HEADS, RET_DV).transpose(0, 2, 1, 3).astype(jnp.float32)
        r_f = _final_state(kc, vc, log_gamma[0])
        r_b = _final_state(flip(kc), flip(vc), log_gamma[1])
    ql, kl, vl, gl = project(h_lat)
    ol_f, _ = _retention_chunks(ql, kl, vl, log_gamma[0], r_f)
    ol_b, _ = _retention_chunks(flip(ql), flip(kl), flip(vl), log_gamma[1], r_b)
    y_lat = readout(ol_f + flip(ol_b), gl)
    return y_lat, y_ctx


def _conv_ffn(h, w_up, conv_w, conv_b, w_down):
    u = h @ w_up
    up = jnp.pad(u, ((0, 0), (1, 1), (0, 0)))
    u = up[:, :-2] * conv_w[0] + up[:, 1:-1] * conv_w[1] + up[:, 2:] * conv_w[2] + conv_b
    a, v = jnp.split(u, 2, axis=-1)
    return (jax.nn.silu(a) * v) @ w_down


def _layer_counts():
    kinds = [i % N_MIXERS for i in range(DEPTH)]
    return kinds.count(MIXER_POOL), kinds.count(MIXER_ATTN), kinds.count(MIXER_RET)


def setup_inputs(seed: int = 0) -> dict:
    key = jax.random.key(seed)
    ks = jax.random.split(key, 22)
    n_pool, n_attn, n_ret = _layer_counts()
    d = D_MODEL

    def nrm(k, shape):
        return jax.random.normal(k, shape, jnp.float32)

    def w(k, shape, fan_in, gain=1.0):
        return nrm(k, shape) * (gain * fan_in ** -0.5)

    def ones_noise(k, shape):
        return 1.0 + 0.05 * nrm(k, shape)

    decay_base = jnp.asarray(np.log(2.0 ** (5 + np.arange(RET_HEADS)) - 1.0).astype(np.float32))
    return {
        'x': nrm(ks[0], (BATCH, SEQ, d)),
        'c': nrm(ks[1], (BATCH, d)),
        'ctx': nrm(ks[2], (BATCH, CTX_LEN, d)),
        'c_ctx': nrm(ks[3], (d,)),
        'ada_w': w(ks[4], (DEPTH, d, 6 * d), d, 0.5),
        'ada_b': 0.01 * nrm(ks[5], (DEPTH, 6 * d)),
        'norm_w': ones_noise(ks[6], (DEPTH, 2, d)),
        'pool_w': w(ks[7], (n_pool, 4, POOL_GROUP, POOL_GROUP), POOL_GROUP),
        'pool_b': 0.01 * nrm(ks[8], (n_pool, d)),
        'pool_scale': ones_noise(ks[9], (n_pool, d)),
        'attn_w_qkv': w(ks[10], (n_attn, d, (ATTN_HEADS + 2 * ATTN_KV_HEADS) * ATTN_HEAD_DIM), d),
        'attn_q_gain': ones_noise(ks[11], (n_attn, ATTN_HEAD_DIM)),
        'attn_k_gain': ones_noise(ks[12], (n_attn, ATTN_HEAD_DIM)),
        'attn_w_o': w(ks[13], (n_attn, ATTN_HEADS * ATTN_HEAD_DIM, d), ATTN_HEADS * ATTN_HEAD_DIM),
        'ret_w_in': w(ks[14], (n_ret, d, 2 * RET_HEADS * RET_DK + 2 * RET_HEADS * RET_DV), d),
        'ret_decay_logit': decay_base[None, None, :] + 0.1 * nrm(ks[15], (n_ret, 2, RET_HEADS)),
        'ret_gn_w': ones_noise(ks[16], (n_ret, RET_HEADS * RET_DV)),
        'ret_w_out': w(ks[17], (n_ret, RET_HEADS * RET_DV, d), RET_HEADS * RET_DV),
        'ffn_w_up': w(ks[18], (DEPTH, d, 2 * D_FF), d),
        'ffn_conv_w': w(ks[19], (DEPTH, CONV_WIDTH, 2 * D_FF), CONV_WIDTH),
        'ffn_conv_b': 0.01 * nrm(ks[20], (DEPTH, 2 * D_FF)),
        'ffn_w_down': w(ks[21], (DEPTH, D_FF, d), D_FF),
    }


def reference(x, c, ctx, c_ctx, ada_w, ada_b, norm_w, pool_w, pool_b, pool_scale,
              attn_w_qkv, attn_q_gain, attn_k_gain, attn_w_o,
              ret_w_in, ret_decay_logit, ret_gn_w, ret_w_out,
              ffn_w_up, ffn_conv_w, ffn_conv_b, ffn_w_down):
    ctx_s = ctx
    silu_c = jax.nn.silu(c)
    silu_cc = jax.nn.silu(c_ctx)
    for i in range(DEPTH):
        kind = i % N_MIXERS
        j = i // N_MIXERS
        need_ctx_out = any(k % N_MIXERS != MIXER_POOL for k in range(i + 1, DEPTH))
        need_ctx_in = need_ctx_out or kind != MIXER_POOL

        sh1, sc1, g1, sh2, sc2, g2 = [m[:, None, :] for m in jnp.split(silu_c @ ada_w[i] + ada_b[i], 6, axis=-1)]
        h = _modulate(_rmsnorm(x, norm_w[i, 0]), sh1, sc1)
        hc = None
        if need_ctx_in:
            csh1, csc1, cg1, csh2, csc2, cg2 = jnp.split(silu_cc @ ada_w[i] + ada_b[i], 6, axis=-1)
            hc = _modulate(_rmsnorm(ctx_s, norm_w[i, 0]), csh1, csc1)

        if kind == MIXER_POOL:
            y = _pool_mixer(h, pool_w[j], pool_b[j], pool_scale[j])
            y_c = _pool_mixer(hc, pool_w[j], pool_b[j], pool_scale[j]) if need_ctx_out else None
        elif kind == MIXER_ATTN:
            y, y_c = _attention_mixer(h, hc, attn_w_qkv[j], attn_q_gain[j], attn_k_gain[j], attn_w_o[j], need_ctx_out)
        else:
            y, y_c = _retention_mixer(h, hc, ret_w_in[j], ret_decay_logit[j], ret_gn_w[j], ret_w_out[j], need_ctx_out)

        x = x + g1 * y
        x = x + g2 * _conv_ffn(_modulate(_rmsnorm(x, norm_w[i, 1]), sh2, sc2),
                               ffn_w_up[i], ffn_conv_w[i], ffn_conv_b[i], ffn_w_down[i])
        if need_ctx_out:
            ctx_s = ctx_s + cg1 * y_c
            ctx_s = ctx_s + cg2 * _conv_ffn(_modulate(_rmsnorm(ctx_s, norm_w[i, 1]), csh2, csc2),
                                            ffn_w_up[i], ffn_conv_w[i], ffn_conv_b[i], ffn_w_down[i])
    return x
```

```python
import functools

import jax
import jax.numpy as jnp
from jax import lax
from jax.experimental import pallas as pl
from jax.experimental.pallas import tpu as pltpu

F32 = jnp.float32
BF16 = jnp.bfloat16

EPS = 1e-6
N_MIXERS = 3
POOL_WINDOWS = (2, 4, 8, 16)
ATTN_HEADS = 8
ATTN_KV_HEADS = 2
ATTN_GROUP = ATTN_HEADS // ATTN_KV_HEADS
GRID_W = 64
ROPE_THETA = 10000.0
RET_HEADS = 4
CONV_WIDTH = 3

HALO = 16
RET_CHUNK = 256
VMEM_LIMIT = 52 * 1024 * 1024


def _cparams(sem):
    return pltpu.CompilerParams(dimension_semantics=sem, vmem_limit_bytes=VMEM_LIMIT)


def _resident(shape):
    nd = len(shape)
    return pl.BlockSpec(shape, lambda *_: (0,) * nd, pipeline_mode=pl.Buffered(1))


def _row_tile(s, want):
    t = min(want, s)
    assert s % t == 0 and t % HALO == 0
    return t


def _sigmoid(x):
    return 1.0 / (1.0 + jnp.exp(-x))


def _norm_mod(x, gain, scale, shift):
    ms = jnp.mean(x * x, axis=-1, keepdims=True)
    return x * lax.rsqrt(ms + EPS) * (gain * (1.0 + scale)) + shift


def _halo_specs(tm, s, d):
    r = tm // HALO
    last = s // HALO - 1
    main = pl.BlockSpec((1, tm, d), lambda b, i: (b, i, 0))
    prev = pl.BlockSpec((1, HALO, d), lambda b, i: (b, jnp.maximum(i * r - 1, 0), 0))
    nxt = pl.BlockSpec((1, HALO, d), lambda b, i: (b, jnp.minimum((i + 1) * r, last), 0))
    return main, prev, nxt


def _vec_spec(d):
    return pl.BlockSpec((1, 1, d), lambda b, i: (b, 0, 0))


def _ada_kernel(c_ref, w_ref, b_ref, o_ref):
    c = c_ref[...]
    s = (c * _sigmoid(c)).astype(BF16)
    o_ref[0] = jnp.dot(s, w_ref[0].astype(BF16), preferred_element_type=F32) + b_ref[0]


def _ada_all(rows, ada_w, ada_b):
    depth, d, n = ada_w.shape
    tn = 1536
    assert n % tn == 0
    return pl.pallas_call(
        _ada_kernel,
        grid=(depth, n // tn),
        in_specs=[
            pl.BlockSpec(rows.shape, lambda l, j: (0, 0)),
            pl.BlockSpec((1, d, tn), lambda l, j: (l, 0, j)),
            pl.BlockSpec((1, 1, tn), lambda l, j: (l, 0, j)),
        ],
        out_specs=pl.BlockSpec((1, rows.shape[0], tn), lambda l, j: (l, 0, j)),
        out_shape=jax.ShapeDtypeStruct((depth, rows.shape[0], n), F32),
        compiler_params=_cparams(("parallel", "parallel")),
        name="ada_mod",
    )(rows, ada_w, ada_b.reshape(depth, 1, n))


def _pool_kernel(x_ref, xp_ref, xn_ref, gain_ref, sh_ref, sc_ref, gt_ref,
                 w_ref, b_ref, ps_ref, o_ref, *, tm, seq):
    i = pl.program_id(1)
    nt = pl.num_programs(1)
    gain, sh, sc, gt = gain_ref[...], sh_ref[0], sc_ref[0], gt_ref[0]
    x = x_ref[0]
    hm = _norm_mod(x, gain, sc, sh)
    hp = jnp.where(i > 0, _norm_mod(xp_ref[0], gain, sc, sh), 0.0)
    hn = jnp.where(i < nt - 1, _norm_mod(xn_ref[0], gain, sc, sh), 0.0)
    hext = jnp.concatenate([hp, hm, hn], axis=0)
    n = tm + 2 * HALO
    t = i * tm + lax.broadcasted_iota(jnp.int32, (tm, 1), 0)
    gw = x.shape[1] // len(POOL_WINDOWS)
    for g, win in enumerate(POOL_WINDOWS):
        sl = slice(g * gw, (g + 1) * gw)
        hg = hext[:, sl]
        acc = hg + pltpu.roll(hg, 1, 0)
        half = 1
        while 2 * half < win:
            acc = pltpu.roll(acc, half, 0) + pltpu.roll(acc, n - half, 0)
            half *= 2
        cnt = jnp.minimum(t + win // 2, seq) - jnp.maximum(t - win // 2, 0)
        mean = acc[HALO:HALO + tm] / cnt.astype(F32)
        dlt = (mean - hm[:, sl]).astype(BF16)
        y = jnp.dot(dlt, w_ref[g], preferred_element_type=F32)
        y = (y + b_ref[:, sl]) * ps_ref[:, sl]
        o_ref[0, :, sl] = x[:, sl] + gt[:, sl] * y


def _pool_layer(x, gain, sh, sc, gt, w, b, ps):
    bsz, s, d = x.shape
    tm = _row_tile(s, 512)
    main, prev, nxt = _halo_specs(tm, s, d)
    vec = _vec_spec(d)
    return pl.pallas_call(
        functools.partial(_pool_kernel, tm=tm, seq=s),
        grid=(bsz, s // tm),
        in_specs=[main, prev, nxt, _resident((1, d)), vec, vec, vec,
                  _resident(w.shape), _resident((1, d)), _resident((1, d))],
        out_specs=main,
        out_shape=jax.ShapeDtypeStruct(x.shape, F32),
        compiler_params=_cparams(("parallel", "parallel")),
        name="pool_layer",
    )(x, x, x, gain, sh, sc, gt, w, b, ps)


def _ffn_kernel(x_ref, xp_ref, xn_ref, gain_ref, sh_ref, sc_ref, gt_ref,
                wup_ref, cw_ref, cb_ref, wdn_ref, o_ref, h_scr, act_scr, *, tm, dff, cw):
    i = pl.program_id(1)
    nt = pl.num_programs(1)
    gain, sh, sc = gain_ref[...], sh_ref[0], sc_ref[0]
    x = x_ref[0]
    n = tm + 2 * HALO
    h_scr[0:HALO] = jnp.where(i > 0, _norm_mod(xp_ref[0], gain, sc, sh), 0.0).astype(BF16)
    h_scr[HALO:HALO + tm] = _norm_mod(x, gain, sc, sh).astype(BF16)
    h_scr[HALO + tm:n] = jnp.where(i < nt - 1, _norm_mod(xn_ref[0], gain, sc, sh), 0.0).astype(BF16)
    def conv(u, col):
        um = pltpu.roll(u, 1, 0)[HALO:HALO + tm]
        up = pltpu.roll(u, n - 1, 0)[HALO:HALO + tm]
        return (um * cw_ref[0:1, col] + u[HALO:HALO + tm] * cw_ref[1:2, col]
                + up * cw_ref[2:3, col] + cb_ref[:, col])

    for j in range(dff // cw):
        ca = slice(j * cw, (j + 1) * cw)
        cv = slice(dff + j * cw, dff + (j + 1) * cw)
        h = h_scr[...]
        ua = jnp.dot(h, wup_ref[:, ca], preferred_element_type=F32)
        uv = jnp.dot(h, wup_ref[:, cv], preferred_element_type=F32)
        a = conv(ua, ca)
        v = conv(uv, cv)
        act_scr[:, ca] = (a * _sigmoid(a) * v).astype(BF16)
    y = jnp.dot(act_scr[...], wdn_ref[...], preferred_element_type=F32)
    o_ref[0] = x + gt_ref[0] * y


def _ffn_layer(x, gain, sh, sc, gt, w_up, conv_w, conv_b, w_dn):
    bsz, s, d = x.shape
    dff = w_dn.shape[0]
    tm = _row_tile(s, 512)
    cw = 256
    assert dff % cw == 0
    main, prev, nxt = _halo_specs(tm, s, d)
    vec = _vec_spec(d)
    return pl.pallas_call(
        functools.partial(_ffn_kernel, tm=tm, dff=dff, cw=cw),
        grid=(bsz, s // tm),
        in_specs=[main, prev, nxt, _resident((1, d)), vec, vec, vec,
                  _resident(w_up.shape), _resident(conv_w.shape), _resident(conv_b.shape),
                  _resident(w_dn.shape)],
        out_specs=main,
        out_shape=jax.ShapeDtypeStruct(x.shape, F32),
        scratch_shapes=[pltpu.VMEM((tm + 2 * HALO, d), BF16), pltpu.VMEM((tm, dff), BF16)],
        compiler_params=_cparams(("parallel", "parallel")),
        name="conv_ffn",
    )(x, x, x, gain, sh, sc, gt, w_up, conv_w, conv_b, w_dn)


def _proj_res_kernel(x_ref, a_ref, gt_ref, w_ref, o_ref):
    y = jnp.dot(a_ref[0], w_ref[...], preferred_element_type=F32)
    o_ref[0] = x_ref[0] + gt_ref[0] * y


def _proj_res(x, a, gt, w):
    bsz, s, d = x.shape
    k = a.shape[-1]
    tm = _row_tile(s, 512)
    return pl.pallas_call(
        _proj_res_kernel,
        grid=(bsz, s // tm),
        in_specs=[pl.BlockSpec((1, tm, d), lambda b, i: (b, i, 0)),
                  pl.BlockSpec((1, tm, k), lambda b, i: (b, i, 0)),
                  _vec_spec(d), _resident(w.shape)],
        out_specs=pl.BlockSpec((1, tm, d), lambda b, i: (b, i, 0)),
        out_shape=jax.ShapeDtypeStruct(x.shape, F32),
        compiler_params=_cparams(("parallel", "parallel")),
        name="proj_residual",
    )(x, a, gt, w)


def _head_norm(xh, gain):
    ms = jnp.mean(xh * xh, axis=-1, keepdims=True)
    return xh * lax.rsqrt(ms + EPS) * gain


def _rope(xh, cos, sin, lane):
    hd = xh.shape[-1]
    q = hd // 4
    swapped = jnp.where((lane % (2 * q)) < q, pltpu.roll(xh, hd - q, 1), pltpu.roll(xh, q, 1))
    return xh * cos + swapped * sin


def _qkv_kernel(*refs, nq, nkv, hd, rope, qscale):
    if rope:
        (x_ref, gain_ref, sh_ref, sc_ref, w_ref, qg_ref, kg_ref, cos_ref, sin_ref,
         q_ref, k_ref, v_ref) = refs
    else:
        x_ref, gain_ref, sh_ref, sc_ref, w_ref, qg_ref, kg_ref, q_ref, k_ref, v_ref = refs
    h = _norm_mod(x_ref[0], gain_ref[...], sc_ref[0], sh_ref[0]).astype(BF16)
    if rope:
        cos, sin = cos_ref[...], sin_ref[...]
        lane = lax.broadcasted_iota(jnp.int32, cos.shape, 1)
    for hh in range(nq + nkv):
        u = jnp.dot(h, w_ref[:, hh * hd:(hh + 1) * hd], preferred_element_type=F32)
        u = _head_norm(u, qg_ref[...] if hh < nq else kg_ref[...])
        if rope:
            u = _rope(u, cos, sin, lane)
        if hh < nq:
            q_ref[0, :, hh * hd:(hh + 1) * hd] = (u * qscale).astype(BF16)
        else:
            k_ref[0, :, (hh - nq) * hd:(hh - nq + 1) * hd] = u.astype(BF16)
    v0 = (nq + nkv) * hd
    v_ref[0] = jnp.dot(h, w_ref[:, v0:v0 + nkv * hd], preferred_element_type=F32).astype(BF16)


def _qkv_proj(x, gain, sh, sc, w, qg, kg, cos, sin):
    bsz, s, d = x.shape
    hd = qg.shape[-1]
    nq, nkv = ATTN_HEADS, ATTN_KV_HEADS
    tm = _row_tile(s, 512)
    rope = cos is not None
    vec = _vec_spec(d)
    row = lambda width: pl.BlockSpec((1, tm, width), lambda b, i: (b, i, 0))
    in_specs = [row(d), _resident((1, d)), vec, vec, _resident(w.shape),
                _resident((1, hd)), _resident((1, hd))]
    args = [x, gain, sh, sc, w, qg, kg]
    if rope:
        tab = pl.BlockSpec((tm, hd), lambda b, i: (i, 0))
        in_specs += [tab, tab]
        args += [cos, sin]
    return pl.pallas_call(
        functools.partial(_qkv_kernel, nq=nq, nkv=nkv, hd=hd, rope=rope, qscale=hd ** -0.5),
        grid=(bsz, s // tm),
        in_specs=in_specs,
        out_specs=[row(nq * hd), row(nkv * hd), row(nkv * hd)],
        out_shape=[jax.ShapeDtypeStruct((bsz, s, nq * hd), BF16),
                   jax.ShapeDtypeStruct((bsz, s, nkv * hd), BF16),
                   jax.ShapeDtypeStruct((bsz, s, nkv * hd), BF16)],
        compiler_params=_cparams(("parallel", "parallel")),
        name="qkv_rope" if rope else "qkv_ctx",
    )(*args)


def _flash_kernel(q_ref, k_ref, v_ref, o_ref, *, tq, tk, hd, grp):
    skv = k_ref.shape[1]
    q = jnp.concatenate([q_ref[0, :, g * hd:(g + 1) * hd] for g in range(grp)], axis=0)
    rows = grp * tq

    def body(t, carry):
        m, l, acc = carry
        off = pl.multiple_of(t * tk, tk)
        kt = k_ref[0, pl.ds(off, tk), :]
        vt = v_ref[0, pl.ds(off, tk), :]
        s = lax.dot_general(q, kt, (((1,), (1,)), ((), ())), preferred_element_type=F32)
        m_new = jnp.maximum(m, jnp.max(s, axis=-1, keepdims=True))
        alpha = jnp.exp(m - m_new)
        p = jnp.exp(s - m_new)
        l = alpha * l + jnp.sum(p, axis=-1, keepdims=True)
        acc = alpha * acc + jnp.dot(p.astype(BF16), vt, preferred_element_type=F32)
        return m_new, l, acc

    init = (jnp.full((rows, 1), -jnp.inf, F32), jnp.zeros((rows, 1), F32),
            jnp.zeros((rows, hd), F32))
    _, l, acc = lax.fori_loop(0, skv // tk, body, init)
    o = acc / l
    for g in range(grp):
        o_ref[0, :, g * hd:(g + 1) * hd] = o[g * tq:(g + 1) * tq].astype(BF16)


def _flash(q, k, v):
    bsz, sq, _ = q.shape
    skv = k.shape[1]
    hd = k.shape[2] // ATTN_KV_HEADS
    grp = ATTN_GROUP
    tq = min(128, sq)
    tk = 256
    assert sq % tq == 0 and skv % tk == 0
    return pl.pallas_call(
        functools.partial(_flash_kernel, tq=tq, tk=tk, hd=hd, grp=grp),
        grid=(bsz, ATTN_KV_HEADS, sq // tq),
        in_specs=[pl.BlockSpec((1, tq, grp * hd), lambda b, kh, i: (b, i, kh)),
                  pl.BlockSpec((1, skv, hd), lambda b, kh, i: (b, 0, kh)),
                  pl.BlockSpec((1, skv, hd), lambda b, kh, i: (b, 0, kh))],
        out_specs=pl.BlockSpec((1, tq, grp * hd), lambda b, kh, i: (b, i, kh)),
        out_shape=jax.ShapeDtypeStruct(q.shape, BF16),
        compiler_params=_cparams(("parallel", "parallel", "arbitrary")),
        name="flash_gqa",
    )(q, k, v)


def _rope_tables(s, hd):
    rows = s // GRID_W
    row = jnp.broadcast_to(jnp.arange(rows)[:, None], (rows, GRID_W)).reshape(-1).astype(F32)
    col = jnp.broadcast_to(jnp.arange(GRID_W)[None, :], (rows, GRID_W)).reshape(-1).astype(F32)
    axis_dim = hd // 2
    inv = ROPE_THETA ** (-jnp.arange(0, axis_dim, 2, dtype=F32) / axis_dim)
    ar, ac = row[:, None] * inv, col[:, None] * inv
    cos = jnp.concatenate([jnp.cos(ar), jnp.cos(ar), jnp.cos(ac), jnp.cos(ac)], axis=-1)
    sin = jnp.concatenate([-jnp.sin(ar), jnp.sin(ar), -jnp.sin(ac), jnp.sin(ac)], axis=-1)
    return cos, sin


def _ret_proj_kernel(x_ref, gain_ref, sh_ref, sc_ref, w_ref, *o_refs, kscale, k_index):
    h = _norm_mod(x_ref[0], gain_ref[...], sc_ref[0], sh_ref[0]).astype(BF16)
    off = 0
    for idx, o_ref in enumerate(o_refs):
        width = o_ref.shape[-1]
        for c0 in range(0, width, 1024):
            c1 = min(c0 + 1024, width)
            u = jnp.dot(h, w_ref[:, off + c0:off + c1], preferred_element_type=F32)
            if idx == k_index:
                u = u * kscale
            o_ref[0, :, c0:c1] = u.astype(o_ref.dtype)
        off += width


def _ret_proj(x, gain, sh, sc, w, widths, dtypes, kscale, k_index):
    bsz, s, d = x.shape
    tm = _row_tile(s, 512)
    vec = _vec_spec(d)
    row = lambda width: pl.BlockSpec((1, tm, width), lambda b, i: (b, i, 0))
    return pl.pallas_call(
        functools.partial(_ret_proj_kernel, kscale=kscale, k_index=k_index),
        grid=(bsz, s // tm),
        in_specs=[row(d), _resident((1, d)), vec, vec, _resident(w.shape)],
        out_specs=[row(wd) for wd in widths],
        out_shape=[jax.ShapeDtypeStruct((bsz, s, wd), dt) for wd, dt in zip(widths, dtypes)],
        compiler_params=_cparams(("parallel", "parallel")),
        name="ret_proj",
    )(x, gain, sh, sc, w)


def _ret_scan_kernel(lg_ref, q_ref, k_ref, v_ref, g_ref, kc_ref, vc_ref, gn_ref, z_ref,
                     r_scr, o_scr, *, nchunk, chunk, heads):
    hh = pl.program_id(1)
    step = pl.program_id(2)
    bwd = step >= nchunk
    lg = jnp.zeros((1, 1), F32) + jnp.where(bwd, lg_ref[heads + hh], lg_ref[hh])
    ctx_len = kc_ref.shape[1]

    def ctx_state(weights):
        kw = (kc_ref[0].astype(F32) * weights).T.astype(BF16)
        r_scr[...] = jnp.dot(kw, vc_ref[0], preferred_element_type=F32)

    jc = lax.broadcasted_iota(jnp.int32, (ctx_len, 1), 0).astype(F32)

    @pl.when(step == 0)
    def _():
        ctx_state(jnp.exp((ctx_len - 1.0 - jc) * lg))

    @pl.when(step == nchunk)
    def _():
        ctx_state(jnp.exp(jc * lg))

    def pos(shape, dim):
        i = lax.broadcasted_iota(jnp.int32, shape, dim)
        return jnp.where(bwd, chunk - 1 - i, i).astype(F32)

    q, k, v = q_ref[0], k_ref[0], v_ref[0]
    diff = pos((chunk, chunk), 0) - pos((chunk, chunk), 1)
    intra = jnp.where(diff >= 0, jnp.exp(jnp.maximum(diff, 0.0) * lg), 0.0)
    pcol = pos((chunk, 1), 0)
    q_dec = jnp.exp((pcol + 1.0) * lg)
    k_dec = jnp.exp((chunk - 1.0 - pcol) * lg)
    chunk_dec = jnp.exp(chunk * lg)

    r = r_scr[...]
    att = lax.dot_general(q, k, (((1,), (1,)), ((), ())), preferred_element_type=F32) * intra
    o = (jnp.dot(att.astype(BF16), v, preferred_element_type=F32)
         + jnp.dot(q, r.astype(BF16), preferred_element_type=F32) * q_dec)
    kd = (k.astype(F32) * k_dec).T.astype(BF16)
    r_scr[...] = r * chunk_dec + jnp.dot(kd, v, preferred_element_type=F32)

    @pl.when(jnp.logical_not(bwd))
    def _():
        o_scr[pl.ds(pl.multiple_of(step * chunk, chunk), chunk), :] = o

    @pl.when(bwd)
    def _():
        c = 2 * nchunk - 1 - step
        y = o_scr[pl.ds(pl.multiple_of(c * chunk, chunk), chunk), :] + o
        mu = jnp.mean(y, axis=-1, keepdims=True)
        yc = y - mu
        var = jnp.mean(yc * yc, axis=-1, keepdims=True)
        yn = yc * lax.rsqrt(var + EPS) * gn_ref[...]
        g = g_ref[0]
        z_ref[0] = (g * _sigmoid(g) * yn).astype(z_ref.dtype)


def _ret_scan(log_gamma, q, k, v, g, kc, vc, gn_w):
    bsz, s, nq = q.shape
    heads = RET_HEADS
    dk, dv = nq // heads, v.shape[-1] // heads
    chunk = min(RET_CHUNK, s)
    assert s % chunk == 0
    n = s // chunk
    ctx_len = kc.shape[1]

    def cidx(st):
        return jnp.where(st < n, st, 2 * n - 1 - st)

    def gidx(st):
        return jnp.where(st < n, n - 1, 2 * n - 1 - st)

    grid_spec = pltpu.PrefetchScalarGridSpec(
        num_scalar_prefetch=1,
        grid=(bsz, heads, 2 * n),
        in_specs=[
            pl.BlockSpec((1, chunk, dk), lambda b, h, st, lg: (b, cidx(st), h)),
            pl.BlockSpec((1, chunk, dk), lambda b, h, st, lg: (b, cidx(st), h)),
            pl.BlockSpec((1, chunk, dv), lambda b, h, st, lg: (b, cidx(st), h)),
            pl.BlockSpec((1, chunk, dv), lambda b, h, st, lg: (b, gidx(st), h)),
            pl.BlockSpec((1, ctx_len, dk), lambda b, h, st, lg: (b, 0, h)),
            pl.BlockSpec((1, ctx_len, dv), lambda b, h, st, lg: (b, 0, h)),
            pl.BlockSpec((1, dv), lambda b, h, st, lg: (0, h)),
        ],
        out_specs=pl.BlockSpec((1, chunk, dv), lambda b, h, st, lg: (b, gidx(st), h)),
        scratch_shapes=[pltpu.VMEM((dk, dv), F32), pltpu.VMEM((s, dv), F32)],
    )
    return pl.pallas_call(
        functools.partial(_ret_scan_kernel, nchunk=n, chunk=chunk, heads=heads),
        grid_spec=grid_spec,
        out_shape=jax.ShapeDtypeStruct(v.shape, BF16),
        compiler_params=_cparams(("parallel", "parallel", "arbitrary")),
        name="ret_scan",
    )(log_gamma, q, k, v, g, kc, vc, gn_w)


def _trunk(x, c, ctx, c_ctx, ada_w, ada_b, norm_w, pool_w, pool_b, pool_scale,
           attn_w_qkv, attn_q_gain, attn_k_gain, attn_w_o,
           ret_w_in, ret_decay_logit, ret_gn_w, ret_w_out,
           ffn_w_up, ffn_conv_w, ffn_conv_b, ffn_w_down):
    bsz, s, d = x.shape
    depth = ada_w.shape[0]
    rows = jnp.concatenate([c, c_ctx[None, :], jnp.zeros((8 - bsz - 1, d), F32)], axis=0)
    mods = _ada_all(rows, ada_w, ada_b)

    def mod_vecs(i):
        m = mods[i].reshape(8, 6, d)
        lat = [m[:bsz, j][:, None, :] for j in range(6)]
        cx = [jnp.broadcast_to(m[bsz, j][None, None, :], (bsz, 1, d)) for j in range(6)]
        return lat, cx

    ctx_s = ctx
    for i in range(depth):
        kind = i % N_MIXERS
        j = i // N_MIXERS
        need_ctx_out = any(k % N_MIXERS != 0 for k in range(i + 1, depth))
        need_ctx_in = need_ctx_out or kind != 0
        (sh1, sc1, g1, sh2, sc2, g2), (csh1, csc1, cg1, csh2, csc2, cg2) = mod_vecs(i)
        gain1, gain2 = norm_w[i, 0][None, :], norm_w[i, 1][None, :]

        if kind == 0:
            pw = pool_w[j].astype(BF16)
            pb, ps = pool_b[j][None, :], pool_scale[j][None, :]
            x = _pool_layer(x, gain1, sh1, sc1, g1, pw, pb, ps)
            if need_ctx_out:
                ctx_s = _pool_layer(ctx_s, gain1, csh1, csc1, cg1, pw, pb, ps)
        elif kind == 1:
            w = attn_w_qkv[j].astype(BF16)
            wo = attn_w_o[j].astype(BF16)
            qg, kg = attn_q_gain[j][None, :], attn_k_gain[j][None, :]
            cos, sin = _rope_tables(s, qg.shape[-1])
            qc, kc, vc = _qkv_proj(ctx_s, gain1, csh1, csc1, w, qg, kg, None, None)
            ql, kl, vl = _qkv_proj(x, gain1, sh1, sc1, w, qg, kg, cos, sin)
            keys = jnp.concatenate([kl, kc], axis=1)
            vals = jnp.concatenate([vl, vc], axis=1)
            x = _proj_res(x, _flash(ql, keys, vals), g1, wo)
            if need_ctx_out:
                ctx_s = _proj_res(ctx_s, _flash(qc, kc, vc), cg1, wo)
        else:
            w_in = ret_w_in[j].astype(BF16)
            w_out = ret_w_out[j].astype(BF16)
            heads = RET_HEADS
            nq = d
            nv = ret_gn_w.shape[-1]
            log_gamma = jax.nn.log_sigmoid(ret_decay_logit[j].astype(F32)).reshape(-1)
            kscale = (nq // heads) ** -0.5
            gn = ret_gn_w[j][None, :]
            assert not need_ctx_out
            kc, vc = _ret_proj(ctx_s, gain1, csh1, csc1, w_in[:, nq:2 * nq + nv],
                               (nq, nv), (BF16, BF16), kscale, 0)
            ql, kl, vl, gl = _ret_proj(x, gain1, sh1, sc1, w_in, (nq, nq, nv, nv),
                                       (BF16, BF16, BF16, F32), kscale, 1)
            z = _ret_scan(log_gamma, ql, kl, vl, gl, kc, vc, gn)
            x = _proj_res(x, z, g1, w_out)

        wu, wd = ffn_w_up[i].astype(BF16), ffn_w_down[i].astype(BF16)
        cw, cb = ffn_conv_w[i], ffn_conv_b[i][None, :]
        x = _ffn_layer(x, gain2, sh2, sc2, g2, wu, cw, cb, wd)
        if need_ctx_out:
            ctx_s = _ffn_layer(ctx_s, gain2, csh2, csc2, cg2, wu, cw, cb, wd)
    return x


def kernel(x, c, ctx, c_ctx, ada_w, ada_b, norm_w, pool_w, pool_b, pool_scale, attn_w_qkv, attn_q_gain, attn_k_gain, attn_w_o, ret_w_in, ret_decay_logit, ret_gn_w, ret_w_out, ffn_w_up, ffn_conv_w, ffn_conv_b, ffn_w_down):
    return _trunk(x, c, ctx, c_ctx, ada_w, ada_b, norm_w, pool_w, pool_b, pool_scale,
                  attn_w_qkv, attn_q_gain, attn_k_gain, attn_w_o,
                  ret_w_in, ret_decay_logit, ret_gn_w, ret_w_out,
                  ffn_w_up, ffn_conv_w, ffn_conv_b, ffn_w_down)
```

```python
import functools

import jax
import jax.numpy as jnp
from jax import lax
from jax.experimental import pallas as pl
from jax.experimental.pallas import tpu as pltpu

F32 = jnp.float32
BF16 = jnp.bfloat16

EPS = 1e-6
LOG2E = 1.4426950408889634
N_MIXERS = 3
POOL_WINDOWS = (2, 4, 8, 16)
ATTN_HEADS = 8
ATTN_KV_HEADS = 2
ATTN_GROUP = ATTN_HEADS // ATTN_KV_HEADS
GRID_W = 64
ROPE_THETA = 10000.0
RET_HEADS = 4
CONV_WIDTH = 3

HALO = 16
RET_CHUNK = 256
VMEM_LIMIT = 52 * 1024 * 1024


def _cparams(sem):
    return pltpu.CompilerParams(dimension_semantics=sem, vmem_limit_bytes=VMEM_LIMIT)


def _resident(shape):
    nd = len(shape)
    return pl.BlockSpec(shape, lambda *_: (0,) * nd, pipeline_mode=pl.Buffered(1))


def _row_tile(s, want):
    t = min(want, s)
    assert s % t == 0 and t % HALO == 0
    return t


def _sigmoid(x):
    return 1.0 / (1.0 + jnp.exp(-x))


def _norm_mod(x, gain, scale, shift):
    ms = jnp.mean(x * x, axis=-1, keepdims=True)
    return x * lax.rsqrt(ms + EPS) * (gain * (1.0 + scale)) + shift


def _halo_specs(tm, s, d):
    r = tm // HALO
    last = s // HALO - 1
    main = pl.BlockSpec((1, tm, d), lambda b, i: (b, i, 0))
    prev = pl.BlockSpec((1, HALO, d), lambda b, i: (b, jnp.maximum(i * r - 1, 0), 0))
    nxt = pl.BlockSpec((1, HALO, d), lambda b, i: (b, jnp.minimum((i + 1) * r, last), 0))
    return main, prev, nxt


def _vec_spec(d):
    return pl.BlockSpec((1, 1, d), lambda b, i: (b, 0, 0))


def _ada_kernel(c_ref, w_ref, b_ref, o_ref):
    c = c_ref[...]
    s = (c * _sigmoid(c)).astype(BF16)
    o_ref[0] = jnp.dot(s, w_ref[0].astype(BF16), preferred_element_type=F32) + b_ref[0]


def _ada_all(rows, ada_w, ada_b):
    depth, d, n = ada_w.shape
    tn = 1536
    assert n % tn == 0
    return pl.pallas_call(
        _ada_kernel,
        grid=(depth, n // tn),
        in_specs=[
            pl.BlockSpec(rows.shape, lambda l, j: (0, 0)),
            pl.BlockSpec((1, d, tn), lambda l, j: (l, 0, j)),
            pl.BlockSpec((1, 1, tn), lambda l, j: (l, 0, j)),
        ],
        out_specs=pl.BlockSpec((1, rows.shape[0], tn), lambda l, j: (l, 0, j)),
        out_shape=jax.ShapeDtypeStruct((depth, rows.shape[0], n), F32),
        compiler_params=_cparams(("parallel", "parallel")),
        name="ada_mod",
    )(rows, ada_w, ada_b.reshape(depth, 1, n))


def _pool_kernel(x_ref, xp_ref, xn_ref, gain_ref, sh_ref, sc_ref, gt_ref,
                 w_ref, b_ref, ps_ref, o_ref, *, tm, seq):
    i = pl.program_id(1)
    nt = pl.num_programs(1)
    gain, sh, sc, gt = gain_ref[...], sh_ref[0], sc_ref[0], gt_ref[0]
    x = x_ref[0]
    hm = _norm_mod(x, gain, sc, sh)
    hp = jnp.where(i > 0, _norm_mod(xp_ref[0], gain, sc, sh), 0.0)
    hn = jnp.where(i < nt - 1, _norm_mod(xn_ref[0], gain, sc, sh), 0.0)
    hext = jnp.concatenate([hp, hm, hn], axis=0)
    n = tm + 2 * HALO
    t = i * tm + lax.broadcasted_iota(jnp.int32, (tm, 1), 0)
    gw = x.shape[1] // len(POOL_WINDOWS)
    for g, win in enumerate(POOL_WINDOWS):
        sl = slice(g * gw, (g + 1) * gw)
        hg = hext[:, sl]
        acc = hg + pltpu.roll(hg, 1, 0)
        half = 1
        while 2 * half < win:
            acc = pltpu.roll(acc, half, 0) + pltpu.roll(acc, n - half, 0)
            half *= 2
        cnt = jnp.minimum(t + win // 2, seq) - jnp.maximum(t - win // 2, 0)
        mean = acc[HALO:HALO + tm] / cnt.astype(F32)
        dlt = (mean - hm[:, sl]).astype(BF16)
        y = jnp.dot(dlt, w_ref[g], preferred_element_type=F32)
        y = (y + b_ref[:, sl]) * ps_ref[:, sl]
        o_ref[0, :, sl] = x[:, sl] + gt[:, sl] * y


def _pool_layer(x, gain, sh, sc, gt, w, b, ps):
    bsz, s, d = x.shape
    tm = _row_tile(s, 512)
    main, prev, nxt = _halo_specs(tm, s, d)
    vec = _vec_spec(d)
    return pl.pallas_call(
        functools.partial(_pool_kernel, tm=tm, seq=s),
        grid=(bsz, s // tm),
        in_specs=[main, prev, nxt, _resident((1, d)), vec, vec, vec,
                  _resident(w.shape), _resident((1, d)), _resident((1, d))],
        out_specs=main,
        out_shape=jax.ShapeDtypeStruct(x.shape, F32),
        compiler_params=_cparams(("parallel", "parallel")),
        name="pool_layer",
    )(x, x, x, gain, sh, sc, gt, w, b, ps)


def _ffn_kernel(x_ref, xp_ref, xn_ref, gain_ref, sh_ref, sc_ref, gt_ref,
                wup_ref, cw_ref, cb_ref, wdn_ref, o_ref, h_scr, act_scr, *, tm, dff, cw):
    i = pl.program_id(1)
    nt = pl.num_programs(1)
    gain, sh, sc = gain_ref[...], sh_ref[0], sc_ref[0]
    x = x_ref[0]
    n = tm + 2 * HALO
    h_scr[0:HALO] = jnp.where(i > 0, _norm_mod(xp_ref[0], gain, sc, sh), 0.0).astype(BF16)
    h_scr[HALO:HALO + tm] = _norm_mod(x, gain, sc, sh).astype(BF16)
    h_scr[HALO + tm:n] = jnp.where(i < nt - 1, _norm_mod(xn_ref[0], gain, sc, sh), 0.0).astype(BF16)
    def conv(u, col):
        um = pltpu.roll(u, 1, 0)[HALO:HALO + tm]
        up = pltpu.roll(u, n - 1, 0)[HALO:HALO + tm]
        return (um * cw_ref[0:1, col] + u[HALO:HALO + tm] * cw_ref[1:2, col]
                + up * cw_ref[2:3, col] + cb_ref[:, col])

    for j in range(dff // cw):
        ca = slice(j * cw, (j + 1) * cw)
        cv = slice(dff + j * cw, dff + (j + 1) * cw)
        h = h_scr[...]
        ua = jnp.dot(h, wup_ref[:, ca], preferred_element_type=F32)
        uv = jnp.dot(h, wup_ref[:, cv], preferred_element_type=F32)
        a = conv(ua, ca)
        v = conv(uv, cv)
        act_scr[:, ca] = (a * _sigmoid(a) * v).astype(BF16)
    y = jnp.dot(act_scr[...], wdn_ref[...], preferred_element_type=F32)
    o_ref[0] = x + gt_ref[0] * y


def _ffn_layer(x, gain, sh, sc, gt, w_up, conv_w, conv_b, w_dn):
    bsz, s, d = x.shape
    dff = w_dn.shape[0]
    tm = _row_tile(s, 512)
    cw = 256
    assert dff % cw == 0
    main, prev, nxt = _halo_specs(tm, s, d)
    vec = _vec_spec(d)
    return pl.pallas_call(
        functools.partial(_ffn_kernel, tm=tm, dff=dff, cw=cw),
        grid=(bsz, s // tm),
        in_specs=[main, prev, nxt, _resident((1, d)), vec, vec, vec,
                  _resident(w_up.shape), _resident(conv_w.shape), _resident(conv_b.shape),
                  _resident(w_dn.shape)],
        out_specs=main,
        out_shape=jax.ShapeDtypeStruct(x.shape, F32),
        scratch_shapes=[pltpu.VMEM((tm + 2 * HALO, d), BF16), pltpu.VMEM((tm, dff), BF16)],
        compiler_params=_cparams(("parallel", "parallel")),
        name="conv_ffn",
    )(x, x, x, gain, sh, sc, gt, w_up, conv_w, conv_b, w_dn)


def _proj_res_kernel(x_ref, a_ref, gt_ref, w_ref, o_ref):
    y = jnp.dot(a_ref[0], w_ref[...], preferred_element_type=F32)
    o_ref[0] = x_ref[0] + gt_ref[0] * y


def _proj_res(x, a, gt, w):
    bsz, s, d = x.shape
    k = a.shape[-1]
    tm = _row_tile(s, 512)
    return pl.pallas_call(
        _proj_res_kernel,
        grid=(bsz, s // tm),
        in_specs=[pl.BlockSpec((1, tm, d), lambda b, i: (b, i, 0)),
                  pl.BlockSpec((1, tm, k), lambda b, i: (b, i, 0)),
                  _vec_spec(d), _resident(w.shape)],
        out_specs=pl.BlockSpec((1, tm, d), lambda b, i: (b, i, 0)),
        out_shape=jax.ShapeDtypeStruct(x.shape, F32),
        compiler_params=_cparams(("parallel", "parallel")),
        name="proj_residual",
    )(x, a, gt, w)


def _head_norm(xh, gain):
    ms = jnp.mean(xh * xh, axis=-1, keepdims=True)
    return xh * lax.rsqrt(ms + EPS) * gain


def _rope(xh, cos, sin, lane):
    hd = xh.shape[-1]
    q = hd // 4
    swapped = jnp.where((lane % (2 * q)) < q, pltpu.roll(xh, hd - q, 1), pltpu.roll(xh, q, 1))
    return xh * cos + swapped * sin


def _qkv_kernel(*refs, nq, nkv, hd, rope, qscale):
    if rope:
        (x_ref, gain_ref, sh_ref, sc_ref, w_ref, wvt_ref, qg_ref, kg_ref, cos_ref, sin_ref,
         q_ref, k_ref, vt_ref) = refs
    else:
        (x_ref, gain_ref, sh_ref, sc_ref, w_ref, wvt_ref, qg_ref, kg_ref,
         q_ref, k_ref, vt_ref) = refs
    h = _norm_mod(x_ref[0], gain_ref[...], sc_ref[0], sh_ref[0]).astype(BF16)
    if rope:
        cos, sin = cos_ref[...], sin_ref[...]
        lane = lax.broadcasted_iota(jnp.int32, cos.shape, 1)
    for hh in range(nq + nkv):
        u = jnp.dot(h, w_ref[:, hh * hd:(hh + 1) * hd], preferred_element_type=F32)
        u = _head_norm(u, qg_ref[...] if hh < nq else kg_ref[...])
        if rope:
            u = _rope(u, cos, sin, lane)
        if hh < nq:
            q_ref[0, :, hh * hd:(hh + 1) * hd] = (u * qscale).astype(BF16)
        else:
            k_ref[0, :, (hh - nq) * hd:(hh - nq + 1) * hd] = u.astype(BF16)
    vt_ref[0] = lax.dot_general(wvt_ref[...], h, (((1,), (1,)), ((), ())),
                                preferred_element_type=F32).astype(BF16)


def _qkv_proj(x, gain, sh, sc, w, wvt, qg, kg, cos, sin):
    bsz, s, d = x.shape
    hd = qg.shape[-1]
    nq, nkv = ATTN_HEADS, ATTN_KV_HEADS
    tm = _row_tile(s, 512)
    rope = cos is not None
    vec = _vec_spec(d)
    row = lambda width: pl.BlockSpec((1, tm, width), lambda b, i: (b, i, 0))
    in_specs = [row(d), _resident((1, d)), vec, vec, _resident(w.shape), _resident(wvt.shape),
                _resident((1, hd)), _resident((1, hd))]
    args = [x, gain, sh, sc, w, wvt, qg, kg]
    if rope:
        tab = pl.BlockSpec((tm, hd), lambda b, i: (i, 0))
        in_specs += [tab, tab]
        args += [cos, sin]
    return pl.pallas_call(
        functools.partial(_qkv_kernel, nq=nq, nkv=nkv, hd=hd, rope=rope,
                          qscale=hd ** -0.5 * LOG2E),
        grid=(bsz, s // tm),
        in_specs=in_specs,
        out_specs=[row(nq * hd), row(nkv * hd),
                   pl.BlockSpec((1, nkv * hd, tm), lambda b, i: (b, 0, i))],
        out_shape=[jax.ShapeDtypeStruct((bsz, s, nq * hd), BF16),
                   jax.ShapeDtypeStruct((bsz, s, nkv * hd), BF16),
                   jax.ShapeDtypeStruct((bsz, nkv * hd, s), BF16)],
        compiler_params=_cparams(("parallel", "parallel")),
        name="qkv_rope" if rope else "qkv_ctx",
    )(*args)


def _flash_kernel(q_ref, k_ref, vt_ref, o_ref, m_scr, l_scr, acc_scr, sa_scr, sb_scr,
                  *, tq, tk, hd, grp):
    n = k_ref.shape[1] // tk
    q = jnp.concatenate([q_ref[0, :, g * hd:(g + 1) * hd] for g in range(grp)], axis=0)
    m_scr[...] = jnp.full(m_scr.shape, -jnp.inf, F32)
    l_scr[...] = jnp.zeros(l_scr.shape, F32)
    acc_scr[...] = jnp.zeros(acc_scr.shape, F32)

    def scores(t, s_scr):
        kt = k_ref[0, pl.ds(pl.multiple_of(t * tk, tk), tk), :]
        s_scr[...] = lax.dot_general(kt, q, (((1,), (1,)), ((), ())),
                                     preferred_element_type=F32)

    def absorb(t, s_scr):
        vt = vt_ref[0, :, pl.ds(pl.multiple_of(t * tk, tk), tk)]
        m_old = m_scr[...]
        m_new = jnp.maximum(m_old, jnp.max(s_scr[...], axis=0, keepdims=True))
        alpha = jnp.exp2(m_old - m_new)
        p = jnp.exp2(s_scr[...] - m_new)
        l_scr[...] = alpha * l_scr[...] + jnp.sum(p, axis=0, keepdims=True)
        acc_scr[...] = alpha * acc_scr[...] + jnp.dot(vt, p.astype(BF16),
                                                      preferred_element_type=F32)
        m_scr[...] = m_new

    scores(0, sa_scr)
    pairs = (n - 1) // 2

    def body(u, carry):
        scores(2 * u + 1, sb_scr)
        absorb(2 * u, sa_scr)
        scores(2 * u + 2, sa_scr)
        absorb(2 * u + 1, sb_scr)
        return carry

    if pairs:
        lax.fori_loop(0, pairs, body, 0)
    if n % 2 == 0:
        scores(n - 1, sb_scr)
        absorb(n - 2, sa_scr)
        absorb(n - 1, sb_scr)
    else:
        absorb(n - 1, sa_scr)
    o = acc_scr[...] / l_scr[...]
    for g in range(grp):
        o_ref[0, :, g * hd:(g + 1) * hd] = o[:, g * tq:(g + 1) * tq].T.astype(BF16)


def _flash(q, k, vt):
    bsz, sq, _ = q.shape
    skv = k.shape[1]
    hd = k.shape[2] // ATTN_KV_HEADS
    grp = ATTN_GROUP
    tq = min(128, sq)
    assert sq % tq == 0 and skv % 128 == 0
    tk = max(t for t in range(128, min(768, skv) + 1, 128) if skv % t == 0)
    return pl.pallas_call(
        functools.partial(_flash_kernel, tq=tq, tk=tk, hd=hd, grp=grp),
        grid=(bsz, ATTN_KV_HEADS, sq // tq),
        in_specs=[pl.BlockSpec((1, tq, grp * hd), lambda b, kh, i: (b, i, kh)),
                  pl.BlockSpec((1, skv, hd), lambda b, kh, i: (b, 0, kh)),
                  pl.BlockSpec((1, hd, skv), lambda b, kh, i: (b, kh, 0))],
        out_specs=pl.BlockSpec((1, tq, grp * hd), lambda b, kh, i: (b, i, kh)),
        out_shape=jax.ShapeDtypeStruct(q.shape, BF16),
        scratch_shapes=[pltpu.VMEM((1, grp * tq), F32), pltpu.VMEM((1, grp * tq), F32),
                        pltpu.VMEM((hd, grp * tq), F32),
                        pltpu.VMEM((tk, grp * tq), F32), pltpu.VMEM((tk, grp * tq), F32)],
        compiler_params=_cparams(("parallel", "parallel", "arbitrary")),
        name="flash_gqa",
    )(q, k, vt)


def _rope_tables(s, hd):
    rows = s // GRID_W
    row = jnp.broadcast_to(jnp.arange(rows)[:, None], (rows, GRID_W)).reshape(-1).astype(F32)
    col = jnp.broadcast_to(jnp.arange(GRID_W)[None, :], (rows, GRID_W)).reshape(-1).astype(F32)
    axis_dim = hd // 2
    inv = ROPE_THETA ** (-jnp.arange(0, axis_dim, 2, dtype=F32) / axis_dim)
    ar, ac = row[:, None] * inv, col[:, None] * inv
    cos = jnp.concatenate([jnp.cos(ar), jnp.cos(ar), jnp.cos(ac), jnp.cos(ac)], axis=-1)
    sin = jnp.concatenate([-jnp.sin(ar), jnp.sin(ar), -jnp.sin(ac), jnp.sin(ac)], axis=-1)
    return cos, sin


def _ret_proj_kernel(x_ref, gain_ref, sh_ref, sc_ref, w_ref, *o_refs, kscale, k_index):
    h = _norm_mod(x_ref[0], gain_ref[...], sc_ref[0], sh_ref[0]).astype(BF16)
    off = 0
    for idx, o_ref in enumerate(o_refs):
        width = o_ref.shape[-1]
        for c0 in range(0, width, 1024):
            c1 = min(c0 + 1024, width)
            u = jnp.dot(h, w_ref[:, off + c0:off + c1], preferred_element_type=F32)
            if idx == k_index:
                u = u * kscale
            o_ref[0, :, c0:c1] = u.astype(o_ref.dtype)
        off += width


def _ret_proj(x, gain, sh, sc, w, widths, dtypes, kscale, k_index):
    bsz, s, d = x.shape
    tm = _row_tile(s, 512)
    vec = _vec_spec(d)
    row = lambda width: pl.BlockSpec((1, tm, width), lambda b, i: (b, i, 0))
    return pl.pallas_call(
        functools.partial(_ret_proj_kernel, kscale=kscale, k_index=k_index),
        grid=(bsz, s // tm),
        in_specs=[row(d), _resident((1, d)), vec, vec, _resident(w.shape)],
        out_specs=[row(wd) for wd in widths],
        out_shape=[jax.ShapeDtypeStruct((bsz, s, wd), dt) for wd, dt in zip(widths, dtypes)],
        compiler_params=_cparams(("parallel", "parallel")),
        name="ret_proj",
    )(x, gain, sh, sc, w)


def _ret_scan_kernel(lg_ref, q_ref, k_ref, v_ref, g_ref, kc_ref, vc_ref, gn_ref, z_ref,
                     r_scr, o_scr, *, nchunk, chunk, heads):
    hh = pl.program_id(1)
    step = pl.program_id(2)
    bwd = step >= nchunk
    lg = jnp.zeros((1, 1), F32) + jnp.where(bwd, lg_ref[heads + hh], lg_ref[hh])
    ctx_len = kc_ref.shape[1]

    def ctx_state(weights):
        kw = (kc_ref[0].astype(F32) * weights).T.astype(BF16)
        r_scr[...] = jnp.dot(kw, vc_ref[0], preferred_element_type=F32)

    jc = lax.broadcasted_iota(jnp.int32, (ctx_len, 1), 0).astype(F32)

    @pl.when(step == 0)
    def _():
        ctx_state(jnp.exp((ctx_len - 1.0 - jc) * lg))

    @pl.when(step == nchunk)
    def _():
        ctx_state(jnp.exp(jc * lg))

    def pos(shape, dim):
        i = lax.broadcasted_iota(jnp.int32, shape, dim)
        return jnp.where(bwd, chunk - 1 - i, i).astype(F32)

    q, k, v = q_ref[0], k_ref[0], v_ref[0]
    diff = pos((chunk, chunk), 0) - pos((chunk, chunk), 1)
    intra = jnp.where(diff >= 0, jnp.exp(jnp.maximum(diff, 0.0) * lg), 0.0)
    pcol = pos((chunk, 1), 0)
    q_dec = jnp.exp((pcol + 1.0) * lg)
    k_dec = jnp.exp((chunk - 1.0 - pcol) * lg)
    chunk_dec = jnp.exp(chunk * lg)

    r = r_scr[...]
    att = lax.dot_general(q, k, (((1,), (1,)), ((), ())), preferred_element_type=F32) * intra
    o = (jnp.dot(att.astype(BF16), v, preferred_element_type=F32)
         + jnp.dot(q, r.astype(BF16), preferred_element_type=F32) * q_dec)
    kd = (k.astype(F32) * k_dec).T.astype(BF16)
    r_scr[...] = r * chunk_dec + jnp.dot(kd, v, preferred_element_type=F32)

    @pl.when(jnp.logical_not(bwd))
    def _():
        o_scr[pl.ds(pl.multiple_of(step * chunk, chunk), chunk), :] = o

    @pl.when(bwd)
    def _():
        c = 2 * nchunk - 1 - step
        y = o_scr[pl.ds(pl.multiple_of(c * chunk, chunk), chunk), :] + o
        mu = jnp.mean(y, axis=-1, keepdims=True)
        yc = y - mu
        var = jnp.mean(yc * yc, axis=-1, keepdims=True)
        yn = yc * lax.rsqrt(var + EPS) * gn_ref[...]
        g = g_ref[0]
        z_ref[0] = (g * _sigmoid(g) * yn).astype(z_ref.dtype)


def _ret_scan(log_gamma, q, k, v, g, kc, vc, gn_w):
    bsz, s, nq = q.shape
    heads = RET_HEADS
    dk, dv = nq // heads, v.shape[-1] // heads
    chunk = min(RET_CHUNK, s)
    assert s % chunk == 0
    n = s // chunk
    ctx_len = kc.shape[1]

    def cidx(st):
        return jnp.where(st < n, st, 2 * n - 1 - st)

    def gidx(st):
        return jnp.where(st < n, n - 1, 2 * n - 1 - st)

    grid_spec = pltpu.PrefetchScalarGridSpec(
        num_scalar_prefetch=1,
        grid=(bsz, heads, 2 * n),
        in_specs=[
            pl.BlockSpec((1, chunk, dk), lambda b, h, st, lg: (b, cidx(st), h)),
            pl.BlockSpec((1, chunk, dk), lambda b, h, st, lg: (b, cidx(st), h)),
            pl.BlockSpec((1, chunk, dv), lambda b, h, st, lg: (b, cidx(st), h)),
            pl.BlockSpec((1, chunk, dv), lambda b, h, st, lg: (b, gidx(st), h)),
            pl.BlockSpec((1, ctx_len, dk), lambda b, h, st, lg: (b, 0, h)),
            pl.BlockSpec((1, ctx_len, dv), lambda b, h, st, lg: (b, 0, h)),
            pl.BlockSpec((1, dv), lambda b, h, st, lg: (0, h)),
        ],
        out_specs=pl.BlockSpec((1, chunk, dv), lambda b, h, st, lg: (b, gidx(st), h)),
        scratch_shapes=[pltpu.VMEM((dk, dv), F32), pltpu.VMEM((s, dv), F32)],
    )
    return pl.pallas_call(
        functools.partial(_ret_scan_kernel, nchunk=n, chunk=chunk, heads=heads),
        grid_spec=grid_spec,
        out_shape=jax.ShapeDtypeStruct(v.shape, BF16),
        compiler_params=_cparams(("parallel", "parallel", "arbitrary")),
        name="ret_scan",
    )(log_gamma, q, k, v, g, kc, vc, gn_w)


def _trunk(x, c, ctx, c_ctx, ada_w, ada_b, norm_w, pool_w, pool_b, pool_scale,
           attn_w_qkv, attn_q_gain, attn_k_gain, attn_w_o,
           ret_w_in, ret_decay_logit, ret_gn_w, ret_w_out,
           ffn_w_up, ffn_conv_w, ffn_conv_b, ffn_w_down):
    bsz, s, d = x.shape
    depth = ada_w.shape[0]
    rows = jnp.concatenate([c, c_ctx[None, :], jnp.zeros((8 - bsz - 1, d), F32)], axis=0)
    mods = _ada_all(rows, ada_w, ada_b)

    def mod_vecs(i):
        m = mods[i].reshape(8, 6, d)
        lat = [m[:bsz, j][:, None, :] for j in range(6)]
        cx = [jnp.broadcast_to(m[bsz, j][None, None, :], (bsz, 1, d)) for j in range(6)]
        return lat, cx

    ctx_s = ctx
    for i in range(depth):
        kind = i % N_MIXERS
        j = i // N_MIXERS
        need_ctx_out = any(k % N_MIXERS != 0 for k in range(i + 1, depth))
        need_ctx_in = need_ctx_out or kind != 0
        (sh1, sc1, g1, sh2, sc2, g2), (csh1, csc1, cg1, csh2, csc2, cg2) = mod_vecs(i)
        gain1, gain2 = norm_w[i, 0][None, :], norm_w[i, 1][None, :]

        if kind == 0:
            pw = pool_w[j].astype(BF16)
            pb, ps = pool_b[j][None, :], pool_scale[j][None, :]
            x = _pool_layer(x, gain1, sh1, sc1, g1, pw, pb, ps)
            if need_ctx_out:
                ctx_s = _pool_layer(ctx_s, gain1, csh1, csc1, cg1, pw, pb, ps)
        elif kind == 1:
            w = attn_w_qkv[j].astype(BF16)
            wo = attn_w_o[j].astype(BF16)
            qg, kg = attn_q_gain[j][None, :], attn_k_gain[j][None, :]
            cos, sin = _rope_tables(s, qg.shape[-1])
            nqk = (ATTN_HEADS + ATTN_KV_HEADS) * qg.shape[-1]
            wvt = w[:, nqk:].T
            qc, kc, vct = _qkv_proj(ctx_s, gain1, csh1, csc1, w, wvt, qg, kg, None, None)
            ql, kl, vlt = _qkv_proj(x, gain1, sh1, sc1, w, wvt, qg, kg, cos, sin)
            keys = jnp.concatenate([kl, kc], axis=1)
            vals_t = jnp.concatenate([vlt, vct], axis=2)
            x = _proj_res(x, _flash(ql, keys, vals_t), g1, wo)
            if need_ctx_out:
                ctx_s = _proj_res(ctx_s, _flash(qc, kc, vct), cg1, wo)
        else:
            w_in = ret_w_in[j].astype(BF16)
            w_out = ret_w_out[j].astype(BF16)
            heads = RET_HEADS
            nq = d
            nv = ret_gn_w.shape[-1]
            log_gamma = jax.nn.log_sigmoid(ret_decay_logit[j].astype(F32)).reshape(-1)
            kscale = (nq // heads) ** -0.5
            gn = ret_gn_w[j][None, :]
            assert not need_ctx_out
            kc, vc = _ret_proj(ctx_s, gain1, csh1, csc1, w_in[:, nq:2 * nq + nv],
                               (nq, nv), (BF16, BF16), kscale, 0)
            ql, kl, vl, gl = _ret_proj(x, gain1, sh1, sc1, w_in, (nq, nq, nv, nv),
                                       (BF16, BF16, BF16, F32), kscale, 1)
            z = _ret_scan(log_gamma, ql, kl, vl, gl, kc, vc, gn)
            x = _proj_res(x, z, g1, w_out)

        wu, wd = ffn_w_up[i].astype(BF16), ffn_w_down[i].astype(BF16)
        cw, cb = ffn_conv_w[i], ffn_conv_b[i][None, :]
        x = _ffn_layer(x, gain2, sh2, sc2, g2, wu, cw, cb, wd)
        if need_ctx_out:
            ctx_s = _ffn_layer(ctx_s, gain2, csh2, csc2, cg2, wu, cw, cb, wd)
    return x


def kernel(x, c, ctx, c_ctx, ada_w, ada_b, norm_w, pool_w, pool_b, pool_scale, attn_w_qkv, attn_q_gain, attn_k_gain, attn_w_o, ret_w_in, ret_decay_logit, ret_gn_w, ret_w_out, ffn_w_up, ffn_conv_w, ffn_conv_b, ffn_w_down):
    return _trunk(x, c, ctx, c_ctx, ada_w, ada_b, norm_w, pool_w, pool_b, pool_scale,
                  attn_w_qkv, attn_q_gain, attn_k_gain, attn_w_o,
                  ret_w_in, ret_decay_logit, ret_gn_w, ret_w_out,
                  ffn_w_up, ffn_conv_w, ffn_conv_b, ffn_w_down)
```

```python
import functools

import jax
import jax.numpy as jnp
from jax import lax
from jax.experimental import pallas as pl
from jax.experimental.pallas import tpu as pltpu

F32 = jnp.float32
BF16 = jnp.bfloat16

EPS = 1e-6
LOG2E = 1.4426950408889634
SAFE_LOG2_LOGIT = 60.0
N_MIXERS = 3
POOL_WINDOWS = (2, 4, 8, 16)
ATTN_HEADS = 8
ATTN_KV_HEADS = 2
ATTN_GROUP = ATTN_HEADS // ATTN_KV_HEADS
GRID_W = 64
ROPE_THETA = 10000.0
RET_HEADS = 4
CONV_WIDTH = 3

HALO = 16
RET_CHUNK = 256
VMEM_LIMIT = 52 * 1024 * 1024


def _cparams(sem):
    return pltpu.CompilerParams(dimension_semantics=sem, vmem_limit_bytes=VMEM_LIMIT)


def _resident(shape):
    nd = len(shape)
    return pl.BlockSpec(shape, lambda *_: (0,) * nd, pipeline_mode=pl.Buffered(1))


def _row_tile(s, want):
    t = min(want, s)
    assert s % t == 0 and t % HALO == 0
    return t


def _sigmoid(x):
    return 1.0 / (1.0 + jnp.exp(-x))


def _norm_mod(x, gain, scale, shift):
    ms = jnp.mean(x * x, axis=-1, keepdims=True)
    return x * lax.rsqrt(ms + EPS) * (gain * (1.0 + scale)) + shift


def _halo_specs(tm, s, d):
    r = tm // HALO
    last = s // HALO - 1
    main = pl.BlockSpec((1, tm, d), lambda b, i: (b, i, 0))
    prev = pl.BlockSpec((1, HALO, d), lambda b, i: (b, jnp.maximum(i * r - 1, 0), 0))
    nxt = pl.BlockSpec((1, HALO, d), lambda b, i: (b, jnp.minimum((i + 1) * r, last), 0))
    return main, prev, nxt


def _vec_spec(d):
    return pl.BlockSpec((1, 1, d), lambda b, i: (b, 0, 0))


def _ada_kernel(c_ref, w_ref, b_ref, o_ref):
    c = c_ref[...]
    s = (c * _sigmoid(c)).astype(BF16)
    o_ref[0] = jnp.dot(s, w_ref[0].astype(BF16), preferred_element_type=F32) + b_ref[0]


def _ada_all(rows, ada_w, ada_b):
    depth, d, n = ada_w.shape
    tn = 1536
    assert n % tn == 0
    return pl.pallas_call(
        _ada_kernel,
        grid=(depth, n // tn),
        in_specs=[
            pl.BlockSpec(rows.shape, lambda l, j: (0, 0)),
            pl.BlockSpec((1, d, tn), lambda l, j: (l, 0, j)),
            pl.BlockSpec((1, 1, tn), lambda l, j: (l, 0, j)),
        ],
        out_specs=pl.BlockSpec((1, rows.shape[0], tn), lambda l, j: (l, 0, j)),
        out_shape=jax.ShapeDtypeStruct((depth, rows.shape[0], n), F32),
        compiler_params=_cparams(("parallel", "parallel")),
        name="ada_mod",
    )(rows, ada_w, ada_b.reshape(depth, 1, n))


def _pool_kernel(x_ref, xp_ref, xn_ref, gain_ref, sh_ref, sc_ref, gt_ref,
                 w_ref, b_ref, ps_ref, o_ref, *, tm, seq):
    i = pl.program_id(1)
    nt = pl.num_programs(1)
    gain, sh, sc, gt = gain_ref[...], sh_ref[0], sc_ref[0], gt_ref[0]
    x = x_ref[0]
    hm = _norm_mod(x, gain, sc, sh)
    hp = jnp.where(i > 0, _norm_mod(xp_ref[0], gain, sc, sh), 0.0)
    hn = jnp.where(i < nt - 1, _norm_mod(xn_ref[0], gain, sc, sh), 0.0)
    hext = jnp.concatenate([hp, hm, hn], axis=0)
    n = tm + 2 * HALO
    t = i * tm + lax.broadcasted_iota(jnp.int32, (tm, 1), 0)
    gw = x.shape[1] // len(POOL_WINDOWS)
    for g, win in enumerate(POOL_WINDOWS):
        sl = slice(g * gw, (g + 1) * gw)
        hg = hext[:, sl]
        acc = hg + pltpu.roll(hg, 1, 0)
        half = 1
        while 2 * half < win:
            acc = pltpu.roll(acc, half, 0) + pltpu.roll(acc, n - half, 0)
            half *= 2
        cnt = jnp.minimum(t + win // 2, seq) - jnp.maximum(t - win // 2, 0)
        mean = acc[HALO:HALO + tm] / cnt.astype(F32)
        dlt = (mean - hm[:, sl]).astype(BF16)
        y = jnp.dot(dlt, w_ref[g], preferred_element_type=F32)
        y = (y + b_ref[:, sl]) * ps_ref[:, sl]
        o_ref[0, :, sl] = x[:, sl] + gt[:, sl] * y


def _pool_layer(x, gain, sh, sc, gt, w, b, ps):
    bsz, s, d = x.shape
    tm = _row_tile(s, 512)
    main, prev, nxt = _halo_specs(tm, s, d)
    vec = _vec_spec(d)
    return pl.pallas_call(
        functools.partial(_pool_kernel, tm=tm, seq=s),
        grid=(bsz, s // tm),
        in_specs=[main, prev, nxt, _resident((1, d)), vec, vec, vec,
                  _resident(w.shape), _resident((1, d)), _resident((1, d))],
        out_specs=main,
        out_shape=jax.ShapeDtypeStruct(x.shape, F32),
        compiler_params=_cparams(("parallel", "parallel")),
        name="pool_layer",
    )(x, x, x, gain, sh, sc, gt, w, b, ps)


def _ffn_kernel(x_ref, xp_ref, xn_ref, gain_ref, sh_ref, sc_ref, gt_ref,
                wup_ref, cw_ref, cb_ref, wdn_ref, o_ref, h_scr, act_scr, *, tm, dff, cw):
    i = pl.program_id(1)
    nt = pl.num_programs(1)
    gain, sh, sc = gain_ref[...], sh_ref[0], sc_ref[0]
    x = x_ref[0]
    n = tm + 2 * HALO
    h_scr[0:HALO] = jnp.where(i > 0, _norm_mod(xp_ref[0], gain, sc, sh), 0.0).astype(BF16)
    h_scr[HALO:HALO + tm] = _norm_mod(x, gain, sc, sh).astype(BF16)
    h_scr[HALO + tm:n] = jnp.where(i < nt - 1, _norm_mod(xn_ref[0], gain, sc, sh), 0.0).astype(BF16)
    def conv(u, col):
        um = pltpu.roll(u, 1, 0)[HALO:HALO + tm]
        up = pltpu.roll(u, n - 1, 0)[HALO:HALO + tm]
        return (um * cw_ref[0:1, col] + u[HALO:HALO + tm] * cw_ref[1:2, col]
                + up * cw_ref[2:3, col] + cb_ref[:, col])

    for j in range(dff // cw):
        ca = slice(j * cw, (j + 1) * cw)
        cv = slice(dff + j * cw, dff + (j + 1) * cw)
        h = h_scr[...]
        ua = jnp.dot(h, wup_ref[:, ca], preferred_element_type=F32)
        uv = jnp.dot(h, wup_ref[:, cv], preferred_element_type=F32)
        a = conv(ua, ca)
        v = conv(uv, cv)
        act_scr[:, ca] = (a * _sigmoid(a) * v).astype(BF16)
    y = jnp.dot(act_scr[...], wdn_ref[...], preferred_element_type=F32)
    o_ref[0] = x + gt_ref[0] * y


def _ffn_layer(x, gain, sh, sc, gt, w_up, conv_w, conv_b, w_dn):
    bsz, s, d = x.shape
    dff = w_dn.shape[0]
    tm = _row_tile(s, 512)
    cw = 256
    assert dff % cw == 0
    main, prev, nxt = _halo_specs(tm, s, d)
    vec = _vec_spec(d)
    return pl.pallas_call(
        functools.partial(_ffn_kernel, tm=tm, dff=dff, cw=cw),
        grid=(bsz, s // tm),
        in_specs=[main, prev, nxt, _resident((1, d)), vec, vec, vec,
                  _resident(w_up.shape), _resident(conv_w.shape), _resident(conv_b.shape),
                  _resident(w_dn.shape)],
        out_specs=main,
        out_shape=jax.ShapeDtypeStruct(x.shape, F32),
        scratch_shapes=[pltpu.VMEM((tm + 2 * HALO, d), BF16), pltpu.VMEM((tm, dff), BF16)],
        compiler_params=_cparams(("parallel", "parallel")),
        name="conv_ffn",
    )(x, x, x, gain, sh, sc, gt, w_up, conv_w, conv_b, w_dn)


def _proj_res_kernel(x_ref, a_ref, gt_ref, w_ref, o_ref):
    y = jnp.dot(a_ref[0], w_ref[...], preferred_element_type=F32)
    o_ref[0] = x_ref[0] + gt_ref[0] * y


def _proj_res(x, a, gt, w):
    bsz, s, d = x.shape
    k = a.shape[-1]
    tm = _row_tile(s, 512)
    return pl.pallas_call(
        _proj_res_kernel,
        grid=(bsz, s // tm),
        in_specs=[pl.BlockSpec((1, tm, d), lambda b, i: (b, i, 0)),
                  pl.BlockSpec((1, tm, k), lambda b, i: (b, i, 0)),
                  _vec_spec(d), _resident(w.shape)],
        out_specs=pl.BlockSpec((1, tm, d), lambda b, i: (b, i, 0)),
        out_shape=jax.ShapeDtypeStruct(x.shape, F32),
        compiler_params=_cparams(("parallel", "parallel")),
        name="proj_residual",
    )(x, a, gt, w)


def _head_norm(xh, gain):
    ms = jnp.mean(xh * xh, axis=-1, keepdims=True)
    return xh * lax.rsqrt(ms + EPS) * gain


def _rope(xh, cos, sin, lane):
    hd = xh.shape[-1]
    q = hd // 4
    swapped = jnp.where((lane % (2 * q)) < q, pltpu.roll(xh, hd - q, 1), pltpu.roll(xh, q, 1))
    return xh * cos + swapped * sin


def _qkv_kernel(*refs, nq, nkv, hd, rope, qscale):
    if rope:
        (x_ref, gain_ref, sh_ref, sc_ref, w_ref, wvt_ref, qg_ref, kg_ref, cos_ref, sin_ref,
         q_ref, k_ref, vt_ref) = refs
    else:
        (x_ref, gain_ref, sh_ref, sc_ref, w_ref, wvt_ref, qg_ref, kg_ref,
         q_ref, k_ref, vt_ref) = refs
    h = _norm_mod(x_ref[0], gain_ref[...], sc_ref[0], sh_ref[0]).astype(BF16)
    if rope:
        cos, sin = cos_ref[...], sin_ref[...]
        lane = lax.broadcasted_iota(jnp.int32, cos.shape, 1)
    for hh in range(nq + nkv):
        u = jnp.dot(h, w_ref[:, hh * hd:(hh + 1) * hd], preferred_element_type=F32)
        u = _head_norm(u, qg_ref[...] if hh < nq else kg_ref[...])
        if rope:
            u = _rope(u, cos, sin, lane)
        if hh < nq:
            q_ref[0, :, hh * hd:(hh + 1) * hd] = (u * qscale).astype(BF16)
        else:
            k_ref[0, :, (hh - nq) * hd:(hh - nq + 1) * hd] = u.astype(BF16)
    vt_ref[0] = lax.dot_general(wvt_ref[...], h, (((1,), (1,)), ((), ())),
                                preferred_element_type=F32).astype(BF16)


def _qkv_proj(x, gain, sh, sc, w, wvt, qg, kg, cos, sin):
    bsz, s, d = x.shape
    hd = qg.shape[-1]
    nq, nkv = ATTN_HEADS, ATTN_KV_HEADS
    tm = _row_tile(s, 512)
    rope = cos is not None
    vec = _vec_spec(d)
    row = lambda width: pl.BlockSpec((1, tm, width), lambda b, i: (b, i, 0))
    in_specs = [row(d), _resident((1, d)), vec, vec, _resident(w.shape), _resident(wvt.shape),
                _resident((1, hd)), _resident((1, hd))]
    args = [x, gain, sh, sc, w, wvt, qg, kg]
    if rope:
        tab = pl.BlockSpec((tm, hd), lambda b, i: (i, 0))
        in_specs += [tab, tab]
        args += [cos, sin]
    return pl.pallas_call(
        functools.partial(_qkv_kernel, nq=nq, nkv=nkv, hd=hd, rope=rope,
                          qscale=hd ** -0.5 * LOG2E),
        grid=(bsz, s // tm),
        in_specs=in_specs,
        out_specs=[row(nq * hd), row(nkv * hd),
                   pl.BlockSpec((1, nkv * hd, tm), lambda b, i: (b, 0, i))],
        out_shape=[jax.ShapeDtypeStruct((bsz, s, nq * hd), BF16),
                   jax.ShapeDtypeStruct((bsz, s, nkv * hd), BF16),
                   jax.ShapeDtypeStruct((bsz, nkv * hd, s), BF16)],
        compiler_params=_cparams(("parallel", "parallel")),
        name="qkv_rope" if rope else "qkv_ctx",
    )(*args)


def _flash_kernel(flag_ref, q_ref, k_ref, vt_ref, o_ref, m_scr, l_scr, acc_scr, p0, p1,
                  *, tq, tk, hd, grp):
    n = k_ref.shape[1] // tk
    q = jnp.concatenate([q_ref[0, :, g * hd:(g + 1) * hd] for g in range(grp)], axis=0)
    l_scr[...] = jnp.zeros(l_scr.shape, F32)
    acc_scr[...] = jnp.zeros(acc_scr.shape, F32)

    def key_slice(t):
        return pl.ds(t * tk, tk) if isinstance(t, int) else pl.ds(pl.multiple_of(t * tk, tk), tk)

    def scores(t):
        return lax.dot_general(k_ref[0, key_slice(t), :], q, (((1,), (1,)), ((), ())),
                               preferred_element_type=F32)

    def produce(t, p_scr):
        p = jnp.exp2(scores(t))
        l_scr[...] += jnp.sum(p, axis=0, keepdims=True)
        p_scr[...] = p.astype(BF16)

    def consume(t, p_scr):
        acc_scr[...] += jnp.dot(vt_ref[0, :, key_slice(t)], p_scr[...],
                                preferred_element_type=F32)

    @pl.when(flag_ref[0] == 1)
    def _():
        produce(0, p0)

        def body(u, carry):
            produce(2 * u + 1, p1)
            consume(2 * u, p0)
            produce(2 * u + 2, p0)
            consume(2 * u + 1, p1)
            return carry

        pairs = (n - 1) // 2
        if pairs:
            lax.fori_loop(0, pairs, body, 0)
        if n % 2 == 0:
            produce(n - 1, p1)
            consume(n - 2, p0)
            consume(n - 1, p1)
        else:
            consume(n - 1, p0)

    @pl.when(flag_ref[0] != 1)
    def _():
        m_scr[...] = jnp.full(m_scr.shape, -jnp.inf, F32)

        def body(t, carry):
            s = scores(t)
            m_old = m_scr[...]
            m_new = jnp.maximum(m_old, jnp.max(s, axis=0, keepdims=True))
            alpha = jnp.exp2(m_old - m_new)
            p = jnp.exp2(s - m_new)
            l_scr[...] = alpha * l_scr[...] + jnp.sum(p, axis=0, keepdims=True)
            acc_scr[...] = alpha * acc_scr[...] + jnp.dot(
                vt_ref[0, :, key_slice(t)], p.astype(BF16), preferred_element_type=F32)
            m_scr[...] = m_new
            return carry

        lax.fori_loop(0, n, body, 0)

    o = acc_scr[...] / l_scr[...]
    for g in range(grp):
        o_ref[0, :, g * hd:(g + 1) * hd] = o[:, g * tq:(g + 1) * tq].T.astype(BF16)


def _flash(bounded, q, k, vt):
    bsz, sq, _ = q.shape
    skv = k.shape[1]
    hd = k.shape[2] // ATTN_KV_HEADS
    grp = ATTN_GROUP
    tq = min(128, sq)
    assert sq % tq == 0 and skv % 128 == 0
    tk = max(t for t in range(128, min(768, skv) + 1, 128) if skv % t == 0)
    grid_spec = pltpu.PrefetchScalarGridSpec(
        num_scalar_prefetch=1,
        grid=(bsz, ATTN_KV_HEADS, sq // tq),
        in_specs=[pl.BlockSpec((1, tq, grp * hd), lambda b, kh, i, f: (b, i, kh)),
                  pl.BlockSpec((1, skv, hd), lambda b, kh, i, f: (b, 0, kh)),
                  pl.BlockSpec((1, hd, skv), lambda b, kh, i, f: (b, kh, 0))],
        out_specs=pl.BlockSpec((1, tq, grp * hd), lambda b, kh, i, f: (b, i, kh)),
        scratch_shapes=[pltpu.VMEM((1, grp * tq), F32), pltpu.VMEM((1, grp * tq), F32),
                        pltpu.VMEM((hd, grp * tq), F32),
                        pltpu.VMEM((tk, grp * tq), BF16), pltpu.VMEM((tk, grp * tq), BF16)],
    )
    return pl.pallas_call(
        functools.partial(_flash_kernel, tq=tq, tk=tk, hd=hd, grp=grp),
        grid_spec=grid_spec,
        out_shape=jax.ShapeDtypeStruct(q.shape, BF16),
        compiler_params=_cparams(("parallel", "parallel", "arbitrary")),
        name="flash_gqa",
    )(bounded, q, k, vt)


def _logits_bounded(qg, kg, hd):
    bound = 1.02 * LOG2E * hd ** 0.5 * jnp.max(jnp.abs(qg)) * jnp.max(jnp.abs(kg))
    return (bound <= SAFE_LOG2_LOGIT).astype(jnp.int32).reshape(1)


def _rope_tables(s, hd):
    rows = s // GRID_W
    row = jnp.broadcast_to(jnp.arange(rows)[:, None], (rows, GRID_W)).reshape(-1).astype(F32)
    col = jnp.broadcast_to(jnp.arange(GRID_W)[None, :], (rows, GRID_W)).reshape(-1).astype(F32)
    axis_dim = hd // 2
    inv = ROPE_THETA ** (-jnp.arange(0, axis_dim, 2, dtype=F32) / axis_dim)
    ar, ac = row[:, None] * inv, col[:, None] * inv
    cos = jnp.concatenate([jnp.cos(ar), jnp.cos(ar), jnp.cos(ac), jnp.cos(ac)], axis=-1)
    sin = jnp.concatenate([-jnp.sin(ar), jnp.sin(ar), -jnp.sin(ac), jnp.sin(ac)], axis=-1)
    return cos, sin


def _ret_proj_kernel(x_ref, gain_ref, sh_ref, sc_ref, w_ref, *o_refs, kscale, k_index):
    h = _norm_mod(x_ref[0], gain_ref[...], sc_ref[0], sh_ref[0]).astype(BF16)
    off = 0
    for idx, o_ref in enumerate(o_refs):
        width = o_ref.shape[-1]
        for c0 in range(0, width, 1024):
            c1 = min(c0 + 1024, width)
            u = jnp.dot(h, w_ref[:, off + c0:off + c1], preferred_element_type=F32)
            if idx == k_index:
                u = u * kscale
            o_ref[0, :, c0:c1] = u.astype(o_ref.dtype)
        off += width


def _ret_proj(x, gain, sh, sc, w, widths, dtypes, kscale, k_index):
    bsz, s, d = x.shape
    tm = _row_tile(s, 512)
    vec = _vec_spec(d)
    row = lambda width: pl.BlockSpec((1, tm, width), lambda b, i: (b, i, 0))
    return pl.pallas_call(
        functools.partial(_ret_proj_kernel, kscale=kscale, k_index=k_index),
        grid=(bsz, s // tm),
        in_specs=[row(d), _resident((1, d)), vec, vec, _resident(w.shape)],
        out_specs=[row(wd) for wd in widths],
        out_shape=[jax.ShapeDtypeStruct((bsz, s, wd), dt) for wd, dt in zip(widths, dtypes)],
        compiler_params=_cparams(("parallel", "parallel")),
        name="ret_proj",
    )(x, gain, sh, sc, w)


def _ret_scan_kernel(lg_ref, q_ref, k_ref, v_ref, g_ref, kc_ref, vc_ref, gn_ref, z_ref,
                     r_scr, o_scr, *, nchunk, chunk, heads):
    hh = pl.program_id(1)
    step = pl.program_id(2)
    bwd = step >= nchunk
    lg = jnp.zeros((1, 1), F32) + jnp.where(bwd, lg_ref[heads + hh], lg_ref[hh])
    ctx_len = kc_ref.shape[1]

    def ctx_state(weights):
        kw = (kc_ref[0].astype(F32) * weights).T.astype(BF16)
        r_scr[...] = jnp.dot(kw, vc_ref[0], preferred_element_type=F32)

    jc = lax.broadcasted_iota(jnp.int32, (ctx_len, 1), 0).astype(F32)

    @pl.when(step == 0)
    def _():
        ctx_state(jnp.exp((ctx_len - 1.0 - jc) * lg))

    @pl.when(step == nchunk)
    def _():
        ctx_state(jnp.exp(jc * lg))

    def pos(shape, dim):
        i = lax.broadcasted_iota(jnp.int32, shape, dim)
        return jnp.where(bwd, chunk - 1 - i, i).astype(F32)

    q, k, v = q_ref[0], k_ref[0], v_ref[0]
    diff = pos((chunk, chunk), 0) - pos((chunk, chunk), 1)
    intra = jnp.where(diff >= 0, jnp.exp(jnp.maximum(diff, 0.0) * lg), 0.0)
    pcol = pos((chunk, 1), 0)
    q_dec = jnp.exp((pcol + 1.0) * lg)
    k_dec = jnp.exp((chunk - 1.0 - pcol) * lg)
    chunk_dec = jnp.exp(chunk * lg)

    r = r_scr[...]
    att = lax.dot_general(q, k, (((1,), (1,)), ((), ())), preferred_element_type=F32) * intra
    o = (jnp.dot(att.astype(BF16), v, preferred_element_type=F32)
         + jnp.dot(q, r.astype(BF16), preferred_element_type=F32) * q_dec)
    kd = (k.astype(F32) * k_dec).T.astype(BF16)
    r_scr[...] = r * chunk_dec + jnp.dot(kd, v, preferred_element_type=F32)

    @pl.when(jnp.logical_not(bwd))
    def _():
        o_scr[pl.ds(pl.multiple_of(step * chunk, chunk), chunk), :] = o

    @pl.when(bwd)
    def _():
        c = 2 * nchunk - 1 - step
        y = o_scr[pl.ds(pl.multiple_of(c * chunk, chunk), chunk), :] + o
        mu = jnp.mean(y, axis=-1, keepdims=True)
        yc = y - mu
        var = jnp.mean(yc * yc, axis=-1, keepdims=True)
        yn = yc * lax.rsqrt(var + EPS) * gn_ref[...]
        g = g_ref[0]
        z_ref[0] = (g * _sigmoid(g) * yn).astype(z_ref.dtype)


def _ret_scan(log_gamma, q, k, v, g, kc, vc, gn_w):
    bsz, s, nq = q.shape
    heads = RET_HEADS
    dk, dv = nq // heads, v.shape[-1] // heads
    chunk = min(RET_CHUNK, s)
    assert s % chunk == 0
    n = s // chunk
    ctx_len = kc.shape[1]

    def cidx(st):
        return jnp.where(st < n, st, 2 * n - 1 - st)

    def gidx(st):
        return jnp.where(st < n, n - 1, 2 * n - 1 - st)

    grid_spec = pltpu.PrefetchScalarGridSpec(
        num_scalar_prefetch=1,
        grid=(bsz, heads, 2 * n),
        in_specs=[
            pl.BlockSpec((1, chunk, dk), lambda b, h, st, lg: (b, cidx(st), h)),
            pl.BlockSpec((1, chunk, dk), lambda b, h, st, lg: (b, cidx(st), h)),
            pl.BlockSpec((1, chunk, dv), lambda b, h, st, lg: (b, cidx(st), h)),
            pl.BlockSpec((1, chunk, dv), lambda b, h, st, lg: (b, gidx(st), h)),
            pl.BlockSpec((1, ctx_len, dk), lambda b, h, st, lg: (b, 0, h)),
            pl.BlockSpec((1, ctx_len, dv), lambda b, h, st, lg: (b, 0, h)),
            pl.BlockSpec((1, dv), lambda b, h, st, lg: (0, h)),
        ],
        out_specs=pl.BlockSpec((1, chunk, dv), lambda b, h, st, lg: (b, gidx(st), h)),
        scratch_shapes=[pltpu.VMEM((dk, dv), F32), pltpu.VMEM((s, dv), F32)],
    )
    return pl.pallas_call(
        functools.partial(_ret_scan_kernel, nchunk=n, chunk=chunk, heads=heads),
        grid_spec=grid_spec,
        out_shape=jax.ShapeDtypeStruct(v.shape, BF16),
        compiler_params=_cparams(("parallel", "parallel", "arbitrary")),
        name="ret_scan",
    )(log_gamma, q, k, v, g, kc, vc, gn_w)


def _trunk(x, c, ctx, c_ctx, ada_w, ada_b, norm_w, pool_w, pool_b, pool_scale,
           attn_w_qkv, attn_q_gain, attn_k_gain, attn_w_o,
           ret_w_in, ret_decay_logit, ret_gn_w, ret_w_out,
           ffn_w_up, ffn_conv_w, ffn_conv_b, ffn_w_down):
    bsz, s, d = x.shape
    depth = ada_w.shape[0]
    rows = jnp.concatenate([c, c_ctx[None, :], jnp.zeros((8 - bsz - 1, d), F32)], axis=0)
    mods = _ada_all(rows, ada_w, ada_b)

    def mod_vecs(i):
        m = mods[i].reshape(8, 6, d)
        lat = [m[:bsz, j][:, None, :] for j in range(6)]
        cx = [jnp.broadcast_to(m[bsz, j][None, None, :], (bsz, 1, d)) for j in range(6)]
        return lat, cx

    ctx_s = ctx
    for i in range(depth):
        kind = i % N_MIXERS
        j = i // N_MIXERS
        need_ctx_out = any(k % N_MIXERS != 0 for k in range(i + 1, depth))
        need_ctx_in = need_ctx_out or kind != 0
        (sh1, sc1, g1, sh2, sc2, g2), (csh1, csc1, cg1, csh2, csc2, cg2) = mod_vecs(i)
        gain1, gain2 = norm_w[i, 0][None, :], norm_w[i, 1][None, :]

        if kind == 0:
            pw = pool_w[j].astype(BF16)
            pb, ps = pool_b[j][None, :], pool_scale[j][None, :]
            x = _pool_layer(x, gain1, sh1, sc1, g1, pw, pb, ps)
            if need_ctx_out:
                ctx_s = _pool_layer(ctx_s, gain1, csh1, csc1, cg1, pw, pb, ps)
        elif kind == 1:
            w = attn_w_qkv[j].astype(BF16)
            wo = attn_w_o[j].astype(BF16)
            qg, kg = attn_q_gain[j][None, :], attn_k_gain[j][None, :]
            cos, sin = _rope_tables(s, qg.shape[-1])
            nqk = (ATTN_HEADS + ATTN_KV_HEADS) * qg.shape[-1]
            wvt = w[:, nqk:].T
            qc, kc, vct = _qkv_proj(ctx_s, gain1, csh1, csc1, w, wvt, qg, kg, None, None)
            ql, kl, vlt = _qkv_proj(x, gain1, sh1, sc1, w, wvt, qg, kg, cos, sin)
            keys = jnp.concatenate([kl, kc], axis=1)
            vals_t = jnp.concatenate([vlt, vct], axis=2)
            bounded = _logits_bounded(qg, kg, qg.shape[-1])
            x = _proj_res(x, _flash(bounded, ql, keys, vals_t), g1, wo)
            if need_ctx_out:
                ctx_s = _proj_res(ctx_s, _flash(bounded, qc, kc, vct), cg1, wo)
        else:
            w_in = ret_w_in[j].astype(BF16)
            w_out = ret_w_out[j].astype(BF16)
            heads = RET_HEADS
            nq = d
            nv = ret_gn_w.shape[-1]
            log_gamma = jax.nn.log_sigmoid(ret_decay_logit[j].astype(F32)).reshape(-1)
            kscale = (nq // heads) ** -0.5
            gn = ret_gn_w[j][None, :]
            assert not need_ctx_out
            kc, vc = _ret_proj(ctx_s, gain1, csh1, csc1, w_in[:, nq:2 * nq + nv],
                               (nq, nv), (BF16, BF16), kscale, 0)
            ql, kl, vl, gl = _ret_proj(x, gain1, sh1, sc1, w_in, (nq, nq, nv, nv),
                                       (BF16, BF16, BF16, F32), kscale, 1)
            z = _ret_scan(log_gamma, ql, kl, vl, gl, kc, vc, gn)
            x = _proj_res(x, z, g1, w_out)

        wu, wd = ffn_w_up[i].astype(BF16), ffn_w_down[i].astype(BF16)
        cw, cb = ffn_conv_w[i], ffn_conv_b[i][None, :]
        x = _ffn_layer(x, gain2, sh2, sc2, g2, wu, cw, cb, wd)
        if need_ctx_out:
            ctx_s = _ffn_layer(ctx_s, gain2, csh2, csc2, cg2, wu, cw, cb, wd)
    return x


def kernel(x, c, ctx, c_ctx, ada_w, ada_b, norm_w, pool_w, pool_b, pool_scale, attn_w_qkv, attn_q_gain, attn_k_gain, attn_w_o, ret_w_in, ret_decay_logit, ret_gn_w, ret_w_out, ffn_w_up, ffn_conv_w, ffn_conv_b, ffn_w_down):
    return _trunk(x, c, ctx, c_ctx, ada_w, ada_b, norm_w, pool_w, pool_b, pool_scale,
                  attn_w_qkv, attn_q_gain, attn_k_gain, attn_w_o,
                  ret_w_in, ret_decay_logit, ret_gn_w, ret_w_out,
                  ffn_w_up, ffn_conv_w, ffn_conv_b, ffn_w_down)
```

```python
import functools

import jax
import jax.numpy as jnp
from jax import lax
from jax.experimental import pallas as pl
from jax.experimental.pallas import tpu as pltpu

F32 = jnp.float32
BF16 = jnp.bfloat16

EPS = 1e-6
LOG2E = 1.4426950408889634
SAFE_LOG2_LOGIT = 60.0
N_MIXERS = 3
POOL_WINDOWS = (2, 4, 8, 16)
ATTN_HEADS = 8
ATTN_KV_HEADS = 2
ATTN_GROUP = ATTN_HEADS // ATTN_KV_HEADS
GRID_W = 64
ROPE_THETA = 10000.0
RET_HEADS = 4
CONV_WIDTH = 3

HALO = 16
RET_CHUNK = 256
RET_BLOCK = 1024
VMEM_LIMIT = 52 * 1024 * 1024


def _cparams(sem):
    return pltpu.CompilerParams(dimension_semantics=sem, vmem_limit_bytes=VMEM_LIMIT)


def _resident(shape):
    nd = len(shape)
    return pl.BlockSpec(shape, lambda *_: (0,) * nd, pipeline_mode=pl.Buffered(1))


def _row_tile(s, want):
    t = min(want, s)
    assert s % t == 0 and t % HALO == 0
    return t


def _sigmoid(x):
    return 1.0 / (1.0 + jnp.exp(-x))


def _norm_mod(x, gain, scale, shift):
    ms = jnp.mean(x * x, axis=-1, keepdims=True)
    return x * lax.rsqrt(ms + EPS) * (gain * (1.0 + scale)) + shift


def _halo_specs(tm, s, d):
    r = tm // HALO
    last = s // HALO - 1
    main = pl.BlockSpec((1, tm, d), lambda b, i: (b, i, 0))
    prev = pl.BlockSpec((1, HALO, d), lambda b, i: (b, jnp.maximum(i * r - 1, 0), 0))
    nxt = pl.BlockSpec((1, HALO, d), lambda b, i: (b, jnp.minimum((i + 1) * r, last), 0))
    return main, prev, nxt


def _vec_spec(d):
    return pl.BlockSpec((1, 1, d), lambda b, i: (b, 0, 0))


def _ada_kernel(c_ref, w_ref, b_ref, o_ref):
    c = c_ref[...]
    s = (c * _sigmoid(c)).astype(BF16)
    o_ref[0] = jnp.dot(s, w_ref[0].astype(BF16), preferred_element_type=F32) + b_ref[0]


def _ada_all(rows, ada_w, ada_b):
    depth, d, n = ada_w.shape
    tn = 1536
    assert n % tn == 0
    return pl.pallas_call(
        _ada_kernel,
        grid=(depth, n // tn),
        in_specs=[
            pl.BlockSpec(rows.shape, lambda l, j: (0, 0)),
            pl.BlockSpec((1, d, tn), lambda l, j: (l, 0, j)),
            pl.BlockSpec((1, 1, tn), lambda l, j: (l, 0, j)),
        ],
        out_specs=pl.BlockSpec((1, rows.shape[0], tn), lambda l, j: (l, 0, j)),
        out_shape=jax.ShapeDtypeStruct((depth, rows.shape[0], n), F32),
        compiler_params=_cparams(("parallel", "parallel")),
        name="ada_mod",
    )(rows, ada_w, ada_b.reshape(depth, 1, n))


def _pool_kernel(x_ref, xp_ref, xn_ref, gain_ref, sh_ref, sc_ref, gt_ref,
                 w_ref, b_ref, ps_ref, o_ref, *, tm, seq):
    i = pl.program_id(1)
    nt = pl.num_programs(1)
    gain, sh, sc, gt = gain_ref[...], sh_ref[0], sc_ref[0], gt_ref[0]
    x = x_ref[0]
    hm = _norm_mod(x, gain, sc, sh)
    hp = jnp.where(i > 0, _norm_mod(xp_ref[0], gain, sc, sh), 0.0)
    hn = jnp.where(i < nt - 1, _norm_mod(xn_ref[0], gain, sc, sh), 0.0)
    hext = jnp.concatenate([hp, hm, hn], axis=0)
    n = tm + 2 * HALO
    t = i * tm + lax.broadcasted_iota(jnp.int32, (tm, 1), 0)
    gw = x.shape[1] // len(POOL_WINDOWS)
    for g, win in enumerate(POOL_WINDOWS):
        sl = slice(g * gw, (g + 1) * gw)
        hg = hext[:, sl]
        acc = hg + pltpu.roll(hg, 1, 0)
        half = 1
        while 2 * half < win:
            acc = pltpu.roll(acc, half, 0) + pltpu.roll(acc, n - half, 0)
            half *= 2
        cnt = jnp.minimum(t + win // 2, seq) - jnp.maximum(t - win // 2, 0)
        mean = acc[HALO:HALO + tm] / cnt.astype(F32)
        dlt = (mean - hm[:, sl]).astype(BF16)
        y = jnp.dot(dlt, w_ref[g], preferred_element_type=F32)
        y = (y + b_ref[:, sl]) * ps_ref[:, sl]
        o_ref[0, :, sl] = x[:, sl] + gt[:, sl] * y


def _pool_layer(x, gain, sh, sc, gt, w, b, ps):
    bsz, s, d = x.shape
    tm = _row_tile(s, 512)
    main, prev, nxt = _halo_specs(tm, s, d)
    vec = _vec_spec(d)
    return pl.pallas_call(
        functools.partial(_pool_kernel, tm=tm, seq=s),
        grid=(bsz, s // tm),
        in_specs=[main, prev, nxt, _resident((1, d)), vec, vec, vec,
                  _resident(w.shape), _resident((1, d)), _resident((1, d))],
        out_specs=main,
        out_shape=jax.ShapeDtypeStruct(x.shape, F32),
        compiler_params=_cparams(("parallel", "parallel")),
        name="pool_layer",
    )(x, x, x, gain, sh, sc, gt, w, b, ps)


def _ffn_kernel(x_ref, xp_ref, xn_ref, gain_ref, sh_ref, sc_ref, gt_ref,
                wup_ref, cw_ref, cb_ref, wdn_ref, o_ref, h_scr, act_scr, *, tm, dff, cw):
    i = pl.program_id(1)
    nt = pl.num_programs(1)
    gain, sh, sc = gain_ref[...], sh_ref[0], sc_ref[0]
    x = x_ref[0]
    n = tm + 2 * HALO
    h_scr[0:HALO] = jnp.where(i > 0, _norm_mod(xp_ref[0], gain, sc, sh), 0.0).astype(BF16)
    h_scr[HALO:HALO + tm] = _norm_mod(x, gain, sc, sh).astype(BF16)
    h_scr[HALO + tm:n] = jnp.where(i < nt - 1, _norm_mod(xn_ref[0], gain, sc, sh), 0.0).astype(BF16)
    def conv(u, col):
        um = pltpu.roll(u, 1, 0)[HALO:HALO + tm]
        up = pltpu.roll(u, n - 1, 0)[HALO:HALO + tm]
        return (um * cw_ref[0:1, col] + u[HALO:HALO + tm] * cw_ref[1:2, col]
                + up * cw_ref[2:3, col] + cb_ref[:, col])

    for j in range(dff // cw):
        ca = slice(j * cw, (j + 1) * cw)
        cv = slice(dff + j * cw, dff + (j + 1) * cw)
        h = h_scr[...]
        ua = jnp.dot(h, wup_ref[:, ca], preferred_element_type=F32)
        uv = jnp.dot(h, wup_ref[:, cv], preferred_element_type=F32)
        a = conv(ua, ca)
        v = conv(uv, cv)
        act_scr[:, ca] = (a * _sigmoid(a) * v).astype(BF16)
    y = jnp.dot(act_scr[...], wdn_ref[...], preferred_element_type=F32)
    o_ref[0] = x + gt_ref[0] * y


def _ffn_layer(x, gain, sh, sc, gt, w_up, conv_w, conv_b, w_dn):
    bsz, s, d = x.shape
    dff = w_dn.shape[0]
    tm = _row_tile(s, 512)
    cw = 256
    assert dff % cw == 0
    main, prev, nxt = _halo_specs(tm, s, d)
    vec = _vec_spec(d)
    return pl.pallas_call(
        functools.partial(_ffn_kernel, tm=tm, dff=dff, cw=cw),
        grid=(bsz, s // tm),
        in_specs=[main, prev, nxt, _resident((1, d)), vec, vec, vec,
                  _resident(w_up.shape), _resident(conv_w.shape), _resident(conv_b.shape),
                  _resident(w_dn.shape)],
        out_specs=main,
        out_shape=jax.ShapeDtypeStruct(x.shape, F32),
        scratch_shapes=[pltpu.VMEM((tm + 2 * HALO, d), BF16), pltpu.VMEM((tm, dff), BF16)],
        compiler_params=_cparams(("parallel", "parallel")),
        name="conv_ffn",
    )(x, x, x, gain, sh, sc, gt, w_up, conv_w, conv_b, w_dn)


def _proj_res_kernel(x_ref, a_ref, gt_ref, w_ref, o_ref):
    y = jnp.dot(a_ref[0], w_ref[...], preferred_element_type=F32)
    o_ref[0] = x_ref[0] + gt_ref[0] * y


def _proj_res(x, a, gt, w):
    bsz, s, d = x.shape
    k = a.shape[-1]
    tm = _row_tile(s, 512)
    return pl.pallas_call(
        _proj_res_kernel,
        grid=(bsz, s // tm),
        in_specs=[pl.BlockSpec((1, tm, d), lambda b, i: (b, i, 0)),
                  pl.BlockSpec((1, tm, k), lambda b, i: (b, i, 0)),
                  _vec_spec(d), _resident(w.shape)],
        out_specs=pl.BlockSpec((1, tm, d), lambda b, i: (b, i, 0)),
        out_shape=jax.ShapeDtypeStruct(x.shape, F32),
        compiler_params=_cparams(("parallel", "parallel")),
        name="proj_residual",
    )(x, a, gt, w)


def _head_norm(xh, gain):
    ms = jnp.mean(xh * xh, axis=-1, keepdims=True)
    return xh * lax.rsqrt(ms + EPS) * gain


def _rope(xh, cos, sin, lane):
    hd = xh.shape[-1]
    q = hd // 4
    swapped = jnp.where((lane % (2 * q)) < q, pltpu.roll(xh, hd - q, 1), pltpu.roll(xh, q, 1))
    return xh * cos + swapped * sin


def _qkv_kernel(*refs, nq, nkv, hd, rope, qscale):
    if rope:
        (x_ref, gain_ref, sh_ref, sc_ref, w_ref, wvt_ref, qg_ref, kg_ref, cos_ref, sin_ref,
         q_ref, k_ref, vt_ref) = refs
    else:
        (x_ref, gain_ref, sh_ref, sc_ref, w_ref, wvt_ref, qg_ref, kg_ref,
         q_ref, k_ref, vt_ref) = refs
    h = _norm_mod(x_ref[0], gain_ref[...], sc_ref[0], sh_ref[0]).astype(BF16)
    if rope:
        cos, sin = cos_ref[...], sin_ref[...]
        lane = lax.broadcasted_iota(jnp.int32, cos.shape, 1)
    for hh in range(nq + nkv):
        u = jnp.dot(h, w_ref[:, hh * hd:(hh + 1) * hd], preferred_element_type=F32)
        u = _head_norm(u, qg_ref[...] if hh < nq else kg_ref[...])
        if rope:
            u = _rope(u, cos, sin, lane)
        if hh < nq:
            q_ref[0, :, hh * hd:(hh + 1) * hd] = (u * qscale).astype(BF16)
        else:
            k_ref[0, :, (hh - nq) * hd:(hh - nq + 1) * hd] = u.astype(BF16)
    vt_ref[0] = lax.dot_general(wvt_ref[...], h, (((1,), (1,)), ((), ())),
                                preferred_element_type=F32).astype(BF16)


def _qkv_proj(x, gain, sh, sc, w, wvt, qg, kg, cos, sin):
    bsz, s, d = x.shape
    hd = qg.shape[-1]
    nq, nkv = ATTN_HEADS, ATTN_KV_HEADS
    tm = _row_tile(s, 512)
    rope = cos is not None
    vec = _vec_spec(d)
    row = lambda width: pl.BlockSpec((1, tm, width), lambda b, i: (b, i, 0))
    in_specs = [row(d), _resident((1, d)), vec, vec, _resident(w.shape), _resident(wvt.shape),
                _resident((1, hd)), _resident((1, hd))]
    args = [x, gain, sh, sc, w, wvt, qg, kg]
    if rope:
        tab = pl.BlockSpec((tm, hd), lambda b, i: (i, 0))
        in_specs += [tab, tab]
        args += [cos, sin]
    return pl.pallas_call(
        functools.partial(_qkv_kernel, nq=nq, nkv=nkv, hd=hd, rope=rope,
                          qscale=hd ** -0.5 * LOG2E),
        grid=(bsz, s // tm),
        in_specs=in_specs,
        out_specs=[row(nq * hd), row(nkv * hd),
                   pl.BlockSpec((1, nkv * hd, tm), lambda b, i: (b, 0, i))],
        out_shape=[jax.ShapeDtypeStruct((bsz, s, nq * hd), BF16),
                   jax.ShapeDtypeStruct((bsz, s, nkv * hd), BF16),
                   jax.ShapeDtypeStruct((bsz, nkv * hd, s), BF16)],
        compiler_params=_cparams(("parallel", "parallel")),
        name="qkv_rope" if rope else "qkv_ctx",
    )(*args)


def _flash_kernel(flag_ref, q_ref, k_ref, vt_ref, o_ref, m_scr, l_scr, acc_scr, p0, p1,
                  *, tq, tk, hd, grp):
    n = k_ref.shape[1] // tk
    q = jnp.concatenate([q_ref[0, :, g * hd:(g + 1) * hd] for g in range(grp)], axis=0)
    l_scr[...] = jnp.zeros(l_scr.shape, F32)
    acc_scr[...] = jnp.zeros(acc_scr.shape, F32)

    def key_slice(t):
        return pl.ds(t * tk, tk) if isinstance(t, int) else pl.ds(pl.multiple_of(t * tk, tk), tk)

    def scores(t):
        return lax.dot_general(k_ref[0, key_slice(t), :], q, (((1,), (1,)), ((), ())),
                               preferred_element_type=F32)

    def produce(t, p_scr):
        p = jnp.exp2(scores(t))
        l_scr[...] += jnp.sum(p, axis=0, keepdims=True)
        p_scr[...] = p.astype(BF16)

    def consume(t, p_scr):
        acc_scr[...] += jnp.dot(vt_ref[0, :, key_slice(t)], p_scr[...],
                                preferred_element_type=F32)

    @pl.when(flag_ref[0] == 1)
    def _():
        produce(0, p0)

        def body(u, carry):
            produce(2 * u + 1, p1)
            consume(2 * u, p0)
            produce(2 * u + 2, p0)
            consume(2 * u + 1, p1)
            return carry

        pairs = (n - 1) // 2
        if pairs:
            lax.fori_loop(0, pairs, body, 0)
        if n % 2 == 0:
            produce(n - 1, p1)
            consume(n - 2, p0)
            consume(n - 1, p1)
        else:
            consume(n - 1, p0)

    @pl.when(flag_ref[0] != 1)
    def _():
        m_scr[...] = jnp.full(m_scr.shape, -jnp.inf, F32)

        def body(t, carry):
            s = scores(t)
            m_old = m_scr[...]
            m_new = jnp.maximum(m_old, jnp.max(s, axis=0, keepdims=True))
            alpha = jnp.exp2(m_old - m_new)
            p = jnp.exp2(s - m_new)
            l_scr[...] = alpha * l_scr[...] + jnp.sum(p, axis=0, keepdims=True)
            acc_scr[...] = alpha * acc_scr[...] + jnp.dot(
                vt_ref[0, :, key_slice(t)], p.astype(BF16), preferred_element_type=F32)
            m_scr[...] = m_new
            return carry

        lax.fori_loop(0, n, body, 0)

    o = acc_scr[...] / l_scr[...]
    for g in range(grp):
        o_ref[0, :, g * hd:(g + 1) * hd] = o[:, g * tq:(g + 1) * tq].T.astype(BF16)


def _flash(bounded, q, k, vt):
    bsz, sq, _ = q.shape
    skv = k.shape[1]
    hd = k.shape[2] // ATTN_KV_HEADS
    grp = ATTN_GROUP
    tq = min(256, sq)
    assert sq % tq == 0 and skv % 128 == 0
    tk = max(t for t in range(128, min(768, skv) + 1, 128) if skv % t == 0)
    grid_spec = pltpu.PrefetchScalarGridSpec(
        num_scalar_prefetch=1,
        grid=(bsz, ATTN_KV_HEADS, sq // tq),
        in_specs=[pl.BlockSpec((1, tq, grp * hd), lambda b, kh, i, f: (b, i, kh)),
                  pl.BlockSpec((1, skv, hd), lambda b, kh, i, f: (b, 0, kh)),
                  pl.BlockSpec((1, hd, skv), lambda b, kh, i, f: (b, kh, 0))],
        out_specs=pl.BlockSpec((1, tq, grp * hd), lambda b, kh, i, f: (b, i, kh)),
        scratch_shapes=[pltpu.VMEM((1, grp * tq), F32), pltpu.VMEM((1, grp * tq), F32),
                        pltpu.VMEM((hd, grp * tq), F32),
                        pltpu.VMEM((tk, grp * tq), BF16), pltpu.VMEM((tk, grp * tq), BF16)],
    )
    return pl.pallas_call(
        functools.partial(_flash_kernel, tq=tq, tk=tk, hd=hd, grp=grp),
        grid_spec=grid_spec,
        out_shape=jax.ShapeDtypeStruct(q.shape, BF16),
        compiler_params=_cparams(("parallel", "parallel", "arbitrary")),
        name="flash_gqa",
    )(bounded, q, k, vt)


def _logits_bounded(qg, kg, hd):
    bound = 1.02 * LOG2E * hd ** 0.5 * jnp.max(jnp.abs(qg)) * jnp.max(jnp.abs(kg))
    return (bound <= SAFE_LOG2_LOGIT).astype(jnp.int32).reshape(1)


def _rope_tables(s, hd):
    rows = s // GRID_W
    row = jnp.broadcast_to(jnp.arange(rows)[:, None], (rows, GRID_W)).reshape(-1).astype(F32)
    col = jnp.broadcast_to(jnp.arange(GRID_W)[None, :], (rows, GRID_W)).reshape(-1).astype(F32)
    axis_dim = hd // 2
    inv = ROPE_THETA ** (-jnp.arange(0, axis_dim, 2, dtype=F32) / axis_dim)
    ar, ac = row[:, None] * inv, col[:, None] * inv
    cos = jnp.concatenate([jnp.cos(ar), jnp.cos(ar), jnp.cos(ac), jnp.cos(ac)], axis=-1)
    sin = jnp.concatenate([-jnp.sin(ar), jnp.sin(ar), -jnp.sin(ac), jnp.sin(ac)], axis=-1)
    return cos, sin


def _ret_proj_kernel(x_ref, gain_ref, sh_ref, sc_ref, w_ref, *o_refs, kscale, k_index):
    h = _norm_mod(x_ref[0], gain_ref[...], sc_ref[0], sh_ref[0]).astype(BF16)
    off = 0
    for idx, o_ref in enumerate(o_refs):
        width = o_ref.shape[-1]
        for c0 in range(0, width, 1024):
            c1 = min(c0 + 1024, width)
            u = jnp.dot(h, w_ref[:, off + c0:off + c1], preferred_element_type=F32)
            if idx == k_index:
                u = u * kscale
            o_ref[0, :, c0:c1] = u.astype(o_ref.dtype)
        off += width


def _ret_proj(x, gain, sh, sc, w, widths, dtypes, kscale, k_index):
    bsz, s, d = x.shape
    tm = _row_tile(s, 512)
    vec = _vec_spec(d)
    row = lambda width: pl.BlockSpec((1, tm, width), lambda b, i: (b, i, 0))
    return pl.pallas_call(
        functools.partial(_ret_proj_kernel, kscale=kscale, k_index=k_index),
        grid=(bsz, s // tm),
        in_specs=[row(d), _resident((1, d)), vec, vec, _resident(w.shape)],
        out_specs=[row(wd) for wd in widths],
        out_shape=[jax.ShapeDtypeStruct((bsz, s, wd), dt) for wd, dt in zip(widths, dtypes)],
        compiler_params=_cparams(("parallel", "parallel")),
        name="ret_proj",
    )(x, gain, sh, sc, w)


def _ret_scan_kernel(lg_ref, q_ref, k_ref, v_ref, g_ref, kc_ref, vc_ref, gn_ref, z_ref,
                     r_scr, o_scr, *, nblk, blk, chunk, heads):
    hh = pl.program_id(1)
    step = pl.program_id(2)
    bwd = step >= nblk
    lg = jnp.zeros((1, 1), F32) + jnp.where(bwd, lg_ref[heads + hh], lg_ref[hh])
    ctx_len = kc_ref.shape[1]
    nsub = blk // chunk

    def ctx_state(weights):
        kw = (kc_ref[0].astype(F32) * weights).T.astype(BF16)
        r_scr[...] = jnp.dot(kw, vc_ref[0], preferred_element_type=F32)

    jc = lax.broadcasted_iota(jnp.int32, (ctx_len, 1), 0).astype(F32)

    @pl.when(step == 0)
    def _():
        ctx_state(jnp.exp((ctx_len - 1.0 - jc) * lg))

    @pl.when(step == nblk)
    def _():
        ctx_state(jnp.exp(jc * lg))

    def pos(shape, dim):
        i = lax.broadcasted_iota(jnp.int32, shape, dim)
        return jnp.where(bwd, chunk - 1 - i, i).astype(F32)

    diff = pos((chunk, chunk), 0) - pos((chunk, chunk), 1)
    intra = jnp.where(diff >= 0, jnp.exp(jnp.maximum(diff, 0.0) * lg), 0.0)
    pcol = pos((chunk, 1), 0)
    q_dec = jnp.exp((pcol + 1.0) * lg)
    k_dec = jnp.exp((chunk - 1.0 - pcol) * lg)
    chunk_dec = jnp.exp(chunk * lg)

    blk_idx = jnp.where(bwd, 2 * nblk - 1 - step, step)
    r = r_scr[...]
    for j in range(nsub):
        row0 = pl.multiple_of(jnp.where(bwd, nsub - 1 - j, j) * chunk, chunk)
        rows = pl.ds(row0, chunk)
        q, k, v = q_ref[0, rows, :], k_ref[0, rows, :], v_ref[0, rows, :]
        att = lax.dot_general(q, k, (((1,), (1,)), ((), ())), preferred_element_type=F32) * intra
        o = (jnp.dot(att.astype(BF16), v, preferred_element_type=F32)
             + jnp.dot(q, r.astype(BF16), preferred_element_type=F32) * q_dec)
        kd = (k.astype(F32) * k_dec).T.astype(BF16)
        r = r * chunk_dec + jnp.dot(kd, v, preferred_element_type=F32)
        grows = pl.ds(pl.multiple_of(blk_idx * blk + row0, chunk), chunk)
        o_scr[grows, :] = jnp.where(bwd, o_scr[grows, :] + o, o)
    r_scr[...] = r

    @pl.when(bwd)
    def _():
        y = o_scr[pl.ds(pl.multiple_of(blk_idx * blk, blk), blk), :]
        mu = jnp.mean(y, axis=-1, keepdims=True)
        yc = y - mu
        var = jnp.mean(yc * yc, axis=-1, keepdims=True)
        yn = yc * lax.rsqrt(var + EPS) * gn_ref[...]
        g = g_ref[0].astype(F32)
        z_ref[0] = (g * _sigmoid(g) * yn).astype(z_ref.dtype)


def _ret_scan(log_gamma, q, k, v, g, kc, vc, gn_w):
    bsz, s, nq = q.shape
    heads = RET_HEADS
    dk, dv = nq // heads, v.shape[-1] // heads
    chunk = min(RET_CHUNK, s)
    blk = min(RET_BLOCK, s)
    assert s % blk == 0 and blk % chunk == 0
    n = s // blk
    ctx_len = kc.shape[1]

    def cidx(st):
        return jnp.where(st < n, st, 2 * n - 1 - st)

    def gidx(st):
        return jnp.where(st < n, n - 1, 2 * n - 1 - st)

    grid_spec = pltpu.PrefetchScalarGridSpec(
        num_scalar_prefetch=1,
        grid=(bsz, heads, 2 * n),
        in_specs=[
            pl.BlockSpec((1, blk, dk), lambda b, h, st, lg: (b, cidx(st), h)),
            pl.BlockSpec((1, blk, dk), lambda b, h, st, lg: (b, cidx(st), h)),
            pl.BlockSpec((1, blk, dv), lambda b, h, st, lg: (b, cidx(st), h)),
            pl.BlockSpec((1, blk, dv), lambda b, h, st, lg: (b, gidx(st), h)),
            pl.BlockSpec((1, ctx_len, dk), lambda b, h, st, lg: (b, 0, h)),
            pl.BlockSpec((1, ctx_len, dv), lambda b, h, st, lg: (b, 0, h)),
            pl.BlockSpec((1, dv), lambda b, h, st, lg: (0, h)),
        ],
        out_specs=pl.BlockSpec((1, blk, dv), lambda b, h, st, lg: (b, gidx(st), h)),
        scratch_shapes=[pltpu.VMEM((dk, dv), F32), pltpu.VMEM((s, dv), F32)],
    )
    return pl.pallas_call(
        functools.partial(_ret_scan_kernel, nblk=n, blk=blk, chunk=chunk, heads=heads),
        grid_spec=grid_spec,
        out_shape=jax.ShapeDtypeStruct(v.shape, BF16),
        compiler_params=_cparams(("parallel", "parallel", "arbitrary")),
        name="ret_scan",
    )(log_gamma, q, k, v, g, kc, vc, gn_w)


def _trunk(x, c, ctx, c_ctx, ada_w, ada_b, norm_w, pool_w, pool_b, pool_scale,
           attn_w_qkv, attn_q_gain, attn_k_gain, attn_w_o,
           ret_w_in, ret_decay_logit, ret_gn_w, ret_w_out,
           ffn_w_up, ffn_conv_w, ffn_conv_b, ffn_w_down):
    bsz, s, d = x.shape
    depth = ada_w.shape[0]
    rows = jnp.concatenate([c, c_ctx[None, :], jnp.zeros((8 - bsz - 1, d), F32)], axis=0)
    mods = _ada_all(rows, ada_w, ada_b)

    def mod_vecs(i):
        m = mods[i].reshape(8, 6, d)
        lat = [m[:bsz, j][:, None, :] for j in range(6)]
        cx = [jnp.broadcast_to(m[bsz, j][None, None, :], (bsz, 1, d)) for j in range(6)]
        return lat, cx

    ctx_s = ctx
    for i in range(depth):
        kind = i % N_MIXERS
        j = i // N_MIXERS
        need_ctx_out = any(k % N_MIXERS != 0 for k in range(i + 1, depth))
        need_ctx_in = need_ctx_out or kind != 0
        (sh1, sc1, g1, sh2, sc2, g2), (csh1, csc1, cg1, csh2, csc2, cg2) = mod_vecs(i)
        gain1, gain2 = norm_w[i, 0][None, :], norm_w[i, 1][None, :]

        if kind == 0:
            pw = pool_w[j].astype(BF16)
            pb, ps = pool_b[j][None, :], pool_scale[j][None, :]
            x = _pool_layer(x, gain1, sh1, sc1, g1, pw, pb, ps)
            if need_ctx_out:
                ctx_s = _pool_layer(ctx_s, gain1, csh1, csc1, cg1, pw, pb, ps)
        elif kind == 1:
            w = attn_w_qkv[j].astype(BF16)
            wo = attn_w_o[j].astype(BF16)
            qg, kg = attn_q_gain[j][None, :], attn_k_gain[j][None, :]
            cos, sin = _rope_tables(s, qg.shape[-1])
            nqk = (ATTN_HEADS + ATTN_KV_HEADS) * qg.shape[-1]
            wvt = w[:, nqk:].T
            qc, kc, vct = _qkv_proj(ctx_s, gain1, csh1, csc1, w, wvt, qg, kg, None, None)
            ql, kl, vlt = _qkv_proj(x, gain1, sh1, sc1, w, wvt, qg, kg, cos, sin)
            keys = jnp.concatenate([kl, kc], axis=1)
            vals_t = jnp.concatenate([vlt, vct], axis=2)
            bounded = _logits_bounded(qg, kg, qg.shape[-1])
            x = _proj_res(x, _flash(bounded, ql, keys, vals_t), g1, wo)
            if need_ctx_out:
                ctx_s = _proj_res(ctx_s, _flash(bounded, qc, kc, vct), cg1, wo)
        else:
            w_in = ret_w_in[j].astype(BF16)
            w_out = ret_w_out[j].astype(BF16)
            heads = RET_HEADS
            nq = d
            nv = ret_gn_w.shape[-1]
            log_gamma = jax.nn.log_sigmoid(ret_decay_logit[j].astype(F32)).reshape(-1)
            kscale = (nq // heads) ** -0.5
            gn = ret_gn_w[j][None, :]
            assert not need_ctx_out
            kc, vc = _ret_proj(ctx_s, gain1, csh1, csc1, w_in[:, nq:2 * nq + nv],
                               (nq, nv), (BF16, BF16), kscale, 0)
            ql, kl, vl, gl = _ret_proj(x, gain1, sh1, sc1, w_in, (nq, nq, nv, nv),
                                       (BF16, BF16, BF16, BF16), kscale, 1)
            z = _ret_scan(log_gamma, ql, kl, vl, gl, kc, vc, gn)
            x = _proj_res(x, z, g1, w_out)

        wu, wd = ffn_w_up[i].astype(BF16), ffn_w_down[i].astype(BF16)
        cw, cb = ffn_conv_w[i], ffn_conv_b[i][None, :]
        x = _ffn_layer(x, gain2, sh2, sc2, g2, wu, cw, cb, wd)
        if need_ctx_out:
            ctx_s = _ffn_layer(ctx_s, gain2, csh2, csc2, cg2, wu, cw, cb, wd)
    return x


def kernel(x, c, ctx, c_ctx, ada_w, ada_b, norm_w, pool_w, pool_b, pool_scale, attn_w_qkv, attn_q_gain, attn_k_gain, attn_w_o, ret_w_in, ret_decay_logit, ret_gn_w, ret_w_out, ffn_w_up, ffn_conv_w, ffn_conv_b, ffn_w_down):
    return _trunk(x, c, ctx, c_ctx, ada_w, ada_b, norm_w, pool_w, pool_b, pool_scale,
                  attn_w_qkv, attn_q_gain, attn_k_gain, attn_w_o,
                  ret_w_in, ret_decay_logit, ret_gn_w, ret_w_out,
                  ffn_w_up, ffn_conv_w, ffn_conv_b, ffn_w_down)
```

```python
import functools

import jax
import jax.numpy as jnp
from jax import lax
from jax.experimental import pallas as pl
from jax.experimental.pallas import tpu as pltpu

F32 = jnp.float32
BF16 = jnp.bfloat16

EPS = 1e-6
LOG2E = 1.4426950408889634
SAFE_LOG2_LOGIT = 60.0
N_MIXERS = 3
POOL_WINDOWS = (2, 4, 8, 16)
ATTN_HEADS = 8
ATTN_KV_HEADS = 2
ATTN_GROUP = ATTN_HEADS // ATTN_KV_HEADS
GRID_W = 64
ROPE_THETA = 10000.0
RET_HEADS = 4
CONV_WIDTH = 3

HALO = 16
RET_CHUNK = 256
RET_BLOCK = 1024
VMEM_LIMIT = 52 * 1024 * 1024


def _cparams(sem):
    return pltpu.CompilerParams(dimension_semantics=sem, vmem_limit_bytes=VMEM_LIMIT)


def _resident(shape):
    nd = len(shape)
    return pl.BlockSpec(shape, lambda *_: (0,) * nd, pipeline_mode=pl.Buffered(1))


def _row_tile(s, want):
    t = min(want, s)
    assert s % t == 0 and t % HALO == 0
    return t


def _sigmoid(x):
    return 1.0 / (1.0 + jnp.exp(-x))


def _norm_mod(x, gain, scale, shift):
    ms = jnp.mean(x * x, axis=-1, keepdims=True)
    return x * lax.rsqrt(ms + EPS) * (gain * (1.0 + scale)) + shift


def _halo_specs(tm, s, d):
    r = tm // HALO
    last = s // HALO - 1
    main = pl.BlockSpec((1, tm, d), lambda b, i: (b, i, 0))
    prev = pl.BlockSpec((1, HALO, d), lambda b, i: (b, jnp.maximum(i * r - 1, 0), 0))
    nxt = pl.BlockSpec((1, HALO, d), lambda b, i: (b, jnp.minimum((i + 1) * r, last), 0))
    return main, prev, nxt


def _vec_spec(d):
    return pl.BlockSpec((1, 1, d), lambda b, i: (b, 0, 0))


def _ada_kernel(c_ref, w_ref, b_ref, o_ref):
    c = c_ref[...]
    s = (c * _sigmoid(c)).astype(BF16)
    o_ref[0] = jnp.dot(s, w_ref[0].astype(BF16), preferred_element_type=F32) + b_ref[0]


def _ada_all(rows, ada_w, ada_b):
    depth, d, n = ada_w.shape
    tn = 1536
    assert n % tn == 0
    return pl.pallas_call(
        _ada_kernel,
        grid=(depth, n // tn),
        in_specs=[
            pl.BlockSpec(rows.shape, lambda l, j: (0, 0)),
            pl.BlockSpec((1, d, tn), lambda l, j: (l, 0, j)),
            pl.BlockSpec((1, 1, tn), lambda l, j: (l, 0, j)),
        ],
        out_specs=pl.BlockSpec((1, rows.shape[0], tn), lambda l, j: (l, 0, j)),
        out_shape=jax.ShapeDtypeStruct((depth, rows.shape[0], n), F32),
        compiler_params=_cparams(("parallel", "parallel")),
        name="ada_mod",
    )(rows, ada_w, ada_b.reshape(depth, 1, n))


def _pool_kernel(x_ref, xp_ref, xn_ref, gain_ref, sh_ref, sc_ref, gt_ref,
                 w_ref, b_ref, ps_ref, o_ref, *, tm, seq):
    i = pl.program_id(1)
    nt = pl.num_programs(1)
    gain, sh, sc, gt = gain_ref[...], sh_ref[0], sc_ref[0], gt_ref[0]
    x = x_ref[0]
    hm = _norm_mod(x, gain, sc, sh)
    hp = jnp.where(i > 0, _norm_mod(xp_ref[0], gain, sc, sh), 0.0)
    hn = jnp.where(i < nt - 1, _norm_mod(xn_ref[0], gain, sc, sh), 0.0)
    hext = jnp.concatenate([hp, hm, hn], axis=0)
    n = tm + 2 * HALO
    t = (i * tm + lax.broadcasted_iota(jnp.int32, (tm, 1), 0)).astype(F32)
    gw = x.shape[1] // len(POOL_WINDOWS)
    for g, win in enumerate(POOL_WINDOWS):
        sl = slice(g * gw, (g + 1) * gw)
        hg = hext[:, sl]
        acc = hg + pltpu.roll(hg, 1, 0)
        half = 1
        while 2 * half < win:
            acc = pltpu.roll(acc, half, 0) + pltpu.roll(acc, n - half, 0)
            half *= 2
        cnt = jnp.minimum(t + win // 2, float(seq)) - jnp.maximum(t - win // 2, 0.0)
        mean = acc[HALO:HALO + tm] * (1.0 / cnt)
        dlt = (mean - hm[:, sl]).astype(BF16)
        y = jnp.dot(dlt, w_ref[g], preferred_element_type=F32)
        y = (y + b_ref[:, sl]) * ps_ref[:, sl]
        o_ref[0, :, sl] = x[:, sl] + gt[:, sl] * y


def _pool_layer(x, gain, sh, sc, gt, w, b, ps):
    bsz, s, d = x.shape
    tm = _row_tile(s, 512)
    main, prev, nxt = _halo_specs(tm, s, d)
    vec = _vec_spec(d)
    return pl.pallas_call(
        functools.partial(_pool_kernel, tm=tm, seq=s),
        grid=(bsz, s // tm),
        in_specs=[main, prev, nxt, _resident((1, d)), vec, vec, vec,
                  _resident(w.shape), _resident((1, d)), _resident((1, d))],
        out_specs=main,
        out_shape=jax.ShapeDtypeStruct(x.shape, F32),
        compiler_params=_cparams(("parallel", "parallel")),
        name="pool_layer",
    )(x, x, x, gain, sh, sc, gt, w, b, ps)


def _ffn_kernel(x_ref, xp_ref, xn_ref, gain_ref, sh_ref, sc_ref, gt_ref,
                wup_ref, cw_ref, cb_ref, wdn_ref, o_ref, h_scr, act_scr, *, tm, dff, cw):
    i = pl.program_id(1)
    nt = pl.num_programs(1)
    gain, sh, sc = gain_ref[...], sh_ref[0], sc_ref[0]
    x = x_ref[0]
    n = tm + 2 * HALO
    h_scr[0:HALO] = jnp.where(i > 0, _norm_mod(xp_ref[0], gain, sc, sh), 0.0).astype(BF16)
    h_scr[HALO:HALO + tm] = _norm_mod(x, gain, sc, sh).astype(BF16)
    h_scr[HALO + tm:n] = jnp.where(i < nt - 1, _norm_mod(xn_ref[0], gain, sc, sh), 0.0).astype(BF16)
    def conv(u, col):
        um = pltpu.roll(u, 1, 0)[HALO:HALO + tm]
        up = pltpu.roll(u, n - 1, 0)[HALO:HALO + tm]
        return (um * cw_ref[0:1, col] + u[HALO:HALO + tm] * cw_ref[1:2, col]
                + up * cw_ref[2:3, col] + cb_ref[:, col])

    for j in range(dff // cw):
        ca = slice(j * cw, (j + 1) * cw)
        cv = slice(dff + j * cw, dff + (j + 1) * cw)
        h = h_scr[...]
        ua = jnp.dot(h, wup_ref[:, ca], preferred_element_type=F32)
        uv = jnp.dot(h, wup_ref[:, cv], preferred_element_type=F32)
        a = conv(ua, ca)
        v = conv(uv, cv)
        act_scr[:, ca] = (a * _sigmoid(a) * v).astype(BF16)
    y = jnp.dot(act_scr[...], wdn_ref[...], preferred_element_type=F32)
    o_ref[0] = x + gt_ref[0] * y


def _ffn_layer(x, gain, sh, sc, gt, w_up, conv_w, conv_b, w_dn):
    bsz, s, d = x.shape
    dff = w_dn.shape[0]
    tm = _row_tile(s, 1024)
    cw = 256
    assert dff % cw == 0
    main, prev, nxt = _halo_specs(tm, s, d)
    vec = _vec_spec(d)
    return pl.pallas_call(
        functools.partial(_ffn_kernel, tm=tm, dff=dff, cw=cw),
        grid=(bsz, s // tm),
        in_specs=[main, prev, nxt, _resident((1, d)), vec, vec, vec,
                  _resident(w_up.shape), _resident(conv_w.shape), _resident(conv_b.shape),
                  _resident(w_dn.shape)],
        out_specs=main,
        out_shape=jax.ShapeDtypeStruct(x.shape, F32),
        scratch_shapes=[pltpu.VMEM((tm + 2 * HALO, d), BF16), pltpu.VMEM((tm, dff), BF16)],
        compiler_params=_cparams(("parallel", "parallel")),
        name="conv_ffn",
    )(x, x, x, gain, sh, sc, gt, w_up, conv_w, conv_b, w_dn)


def _proj_res_kernel(x_ref, a_ref, gt_ref, w_ref, o_ref):
    y = jnp.dot(a_ref[0], w_ref[...], preferred_element_type=F32)
    o_ref[0] = x_ref[0] + gt_ref[0] * y


def _proj_res(x, a, gt, w):
    bsz, s, d = x.shape
    k = a.shape[-1]
    tm = _row_tile(s, 512)
    return pl.pallas_call(
        _proj_res_kernel,
        grid=(bsz, s // tm),
        in_specs=[pl.BlockSpec((1, tm, d), lambda b, i: (b, i, 0)),
                  pl.BlockSpec((1, tm, k), lambda b, i: (b, i, 0)),
                  _vec_spec(d), _resident(w.shape)],
        out_specs=pl.BlockSpec((1, tm, d), lambda b, i: (b, i, 0)),
        out_shape=jax.ShapeDtypeStruct(x.shape, F32),
        compiler_params=_cparams(("parallel", "parallel")),
        name="proj_residual",
    )(x, a, gt, w)


def _head_norm(xh, gain):
    ms = jnp.mean(xh * xh, axis=-1, keepdims=True)
    return xh * lax.rsqrt(ms + EPS) * gain


def _rope(xh, cos, sin):
    return xh * cos + pltpu.roll(xh, xh.shape[-1] // 2, 1) * sin


def _rope_perm(hd):
    q = hd // 4
    return jnp.concatenate([jnp.arange(0, q), jnp.arange(2 * q, 3 * q),
                            jnp.arange(q, 2 * q), jnp.arange(3 * q, 4 * q)])


def _qkv_kernel(*refs, nq, nkv, hd, rope, qscale):
    if rope:
        (x_ref, gain_ref, sh_ref, sc_ref, w_ref, wvt_ref, qg_ref, kg_ref, cos_ref, sin_ref,
         q_ref, k_ref, vt_ref) = refs
    else:
        (x_ref, gain_ref, sh_ref, sc_ref, w_ref, wvt_ref, qg_ref, kg_ref,
         q_ref, k_ref, vt_ref) = refs
    h = _norm_mod(x_ref[0], gain_ref[...], sc_ref[0], sh_ref[0]).astype(BF16)
    if rope:
        cos, sin = cos_ref[...], sin_ref[...]
    for pair in range((nq + nkv) // 2):
        u2 = jnp.dot(h, w_ref[:, 2 * pair * hd:(2 * pair + 2) * hd], preferred_element_type=F32)
        for hh in (2 * pair, 2 * pair + 1):
            u = u2[:, (hh - 2 * pair) * hd:(hh - 2 * pair + 1) * hd]
            u = _head_norm(u, qg_ref[...] if hh < nq else kg_ref[...])
            if rope:
                u = _rope(u, cos, sin)
            if hh < nq:
                q_ref[0, :, hh * hd:(hh + 1) * hd] = (u * qscale).astype(BF16)
            else:
                k_ref[0, :, (hh - nq) * hd:(hh - nq + 1) * hd] = u.astype(BF16)
    vt_ref[0] = lax.dot_general(wvt_ref[...], h, (((1,), (1,)), ((), ())),
                                preferred_element_type=F32).astype(BF16)


def _qkv_proj(x, gain, sh, sc, w, wvt, qg, kg, cos, sin):
    bsz, s, d = x.shape
    hd = qg.shape[-1]
    nq, nkv = ATTN_HEADS, ATTN_KV_HEADS
    tm = _row_tile(s, 512)
    rope = cos is not None
    vec = _vec_spec(d)
    row = lambda width: pl.BlockSpec((1, tm, width), lambda b, i: (b, i, 0))
    in_specs = [row(d), _resident((1, d)), vec, vec, _resident(w.shape), _resident(wvt.shape),
                _resident((1, hd)), _resident((1, hd))]
    args = [x, gain, sh, sc, w, wvt, qg, kg]
    if rope:
        tab = pl.BlockSpec((tm, hd), lambda b, i: (i, 0))
        in_specs += [tab, tab]
        args += [cos, sin]
    return pl.pallas_call(
        functools.partial(_qkv_kernel, nq=nq, nkv=nkv, hd=hd, rope=rope,
                          qscale=hd ** -0.5 * LOG2E),
        grid=(bsz, s // tm),
        in_specs=in_specs,
        out_specs=[row(nq * hd), row(nkv * hd),
                   pl.BlockSpec((1, nkv * hd, tm), lambda b, i: (b, 0, i))],
        out_shape=[jax.ShapeDtypeStruct((bsz, s, nq * hd), BF16),
                   jax.ShapeDtypeStruct((bsz, s, nkv * hd), BF16),
                   jax.ShapeDtypeStruct((bsz, nkv * hd, s), BF16)],
        compiler_params=_cparams(("parallel", "parallel")),
        name="qkv_rope" if rope else "qkv_ctx",
    )(*args)


def _flash_kernel(flag_ref, q_ref, k_ref, vt_ref, o_ref, m_scr, l_scr, acc_scr, p0, p1,
                  *, tq, tk, hd, grp):
    n = k_ref.shape[1] // tk
    q = jnp.concatenate([q_ref[0, :, g * hd:(g + 1) * hd] for g in range(grp)], axis=0)
    l_scr[...] = jnp.zeros(l_scr.shape, F32)
    acc_scr[...] = jnp.zeros(acc_scr.shape, F32)

    def key_slice(t):
        return pl.ds(t * tk, tk) if isinstance(t, int) else pl.ds(pl.multiple_of(t * tk, tk), tk)

    def scores(t):
        return lax.dot_general(k_ref[0, key_slice(t), :], q, (((1,), (1,)), ((), ())),
                               preferred_element_type=F32)

    def produce(t, p_scr):
        p = jnp.exp2(scores(t))
        l_scr[...] += jnp.sum(p, axis=0, keepdims=True)
        p_scr[...] = p.astype(BF16)

    def consume(t, p_scr):
        acc_scr[...] += jnp.dot(vt_ref[0, :, key_slice(t)], p_scr[...],
                                preferred_element_type=F32)

    @pl.when(flag_ref[0] == 1)
    def _():
        produce(0, p0)

        def body(u, carry):
            produce(2 * u + 1, p1)
            consume(2 * u, p0)
            produce(2 * u + 2, p0)
            consume(2 * u + 1, p1)
            return carry

        pairs = (n - 1) // 2
        if pairs:
            lax.fori_loop(0, pairs, body, 0)
        if n % 2 == 0:
            produce(n - 1, p1)
            consume(n - 2, p0)
            consume(n - 1, p1)
        else:
            consume(n - 1, p0)

    @pl.when(flag_ref[0] != 1)
    def _():
        m_scr[...] = jnp.full(m_scr.shape, -jnp.inf, F32)

        def body(t, carry):
            s = scores(t)
            m_old = m_scr[...]
            m_new = jnp.maximum(m_old, jnp.max(s, axis=0, keepdims=True))
            alpha = jnp.exp2(m_old - m_new)
            p = jnp.exp2(s - m_new)
            l_scr[...] = alpha * l_scr[...] + jnp.sum(p, axis=0, keepdims=True)
            acc_scr[...] = alpha * acc_scr[...] + jnp.dot(
                vt_ref[0, :, key_slice(t)], p.astype(BF16), preferred_element_type=F32)
            m_scr[...] = m_new
            return carry

        lax.fori_loop(0, n, body, 0)

    o = acc_scr[...] / l_scr[...]
    for g in range(grp):
        o_ref[0, :, g * hd:(g + 1) * hd] = o[:, g * tq:(g + 1) * tq].T.astype(BF16)


def _flash(bounded, q, k, vt):
    bsz, sq, _ = q.shape
    skv = k.shape[1]
    hd = k.shape[2] // ATTN_KV_HEADS
    grp = ATTN_GROUP
    tq = min(256, sq)
    assert sq % tq == 0 and skv % 128 == 0
    tk = max(t for t in range(128, min(768, skv) + 1, 128) if skv % t == 0)
    grid_spec = pltpu.PrefetchScalarGridSpec(
        num_scalar_prefetch=1,
        grid=(bsz, ATTN_KV_HEADS, sq // tq),
        in_specs=[pl.BlockSpec((1, tq, grp * hd), lambda b, kh, i, f: (b, i, kh)),
                  pl.BlockSpec((1, skv, hd), lambda b, kh, i, f: (b, 0, kh)),
                  pl.BlockSpec((1, hd, skv), lambda b, kh, i, f: (b, kh, 0))],
        out_specs=pl.BlockSpec((1, tq, grp * hd), lambda b, kh, i, f: (b, i, kh)),
        scratch_shapes=[pltpu.VMEM((1, grp * tq), F32), pltpu.VMEM((1, grp * tq), F32),
                        pltpu.VMEM((hd, grp * tq), F32),
                        pltpu.VMEM((tk, grp * tq), BF16), pltpu.VMEM((tk, grp * tq), BF16)],
    )
    return pl.pallas_call(
        functools.partial(_flash_kernel, tq=tq, tk=tk, hd=hd, grp=grp),
        grid_spec=grid_spec,
        out_shape=jax.ShapeDtypeStruct(q.shape, BF16),
        compiler_params=_cparams(("parallel", "parallel", "arbitrary")),
        name="flash_gqa",
    )(bounded, q, k, vt)


def _logits_bounded(qg, kg, hd):
    bound = 1.02 * LOG2E * hd ** 0.5 * jnp.max(jnp.abs(qg)) * jnp.max(jnp.abs(kg))
    return (bound <= SAFE_LOG2_LOGIT).astype(jnp.int32).reshape(1)


def _rope_tables(s, hd):
    rows = s // GRID_W
    row = jnp.broadcast_to(jnp.arange(rows)[:, None], (rows, GRID_W)).reshape(-1).astype(F32)
    col = jnp.broadcast_to(jnp.arange(GRID_W)[None, :], (rows, GRID_W)).reshape(-1).astype(F32)
    axis_dim = hd // 2
    inv = ROPE_THETA ** (-jnp.arange(0, axis_dim, 2, dtype=F32) / axis_dim)
    ar, ac = row[:, None] * inv, col[:, None] * inv
    cos = jnp.concatenate([jnp.cos(ar), jnp.cos(ac), jnp.cos(ar), jnp.cos(ac)], axis=-1)
    sin = jnp.concatenate([-jnp.sin(ar), -jnp.sin(ac), jnp.sin(ar), jnp.sin(ac)], axis=-1)
    return cos, sin


def _ret_proj_kernel(x_ref, gain_ref, sh_ref, sc_ref, w_ref, *o_refs, kscale, k_index):
    h = _norm_mod(x_ref[0], gain_ref[...], sc_ref[0], sh_ref[0]).astype(BF16)
    off = 0
    for idx, o_ref in enumerate(o_refs):
        width = o_ref.shape[-1]
        for c0 in range(0, width, 1024):
            c1 = min(c0 + 1024, width)
            u = jnp.dot(h, w_ref[:, off + c0:off + c1], preferred_element_type=F32)
            if idx == k_index:
                u = u * kscale
            o_ref[0, :, c0:c1] = u.astype(o_ref.dtype)
        off += width


def _ret_proj(x, gain, sh, sc, w, widths, dtypes, kscale, k_index):
    bsz, s, d = x.shape
    tm = _row_tile(s, 512)
    vec = _vec_spec(d)
    row = lambda width: pl.BlockSpec((1, tm, width), lambda b, i: (b, i, 0))
    return pl.pallas_call(
        functools.partial(_ret_proj_kernel, kscale=kscale, k_index=k_index),
        grid=(bsz, s // tm),
        in_specs=[row(d), _resident((1, d)), vec, vec, _resident(w.shape)],
        out_specs=[row(wd) for wd in widths],
        out_shape=[jax.ShapeDtypeStruct((bsz, s, wd), dt) for wd, dt in zip(widths, dtypes)],
        compiler_params=_cparams(("parallel", "parallel")),
        name="ret_proj",
    )(x, gain, sh, sc, w)


def _ret_scan_kernel(lg_ref, q_ref, k_ref, v_ref, g_ref, kc_ref, vc_ref, gn_ref, z_ref,
                     r_scr, o_scr, *, nblk, blk, chunk, heads):
    hh = pl.program_id(1)
    step = pl.program_id(2)
    bwd = step >= nblk
    lg = jnp.zeros((1, 1), F32) + jnp.where(bwd, lg_ref[heads + hh], lg_ref[hh])
    ctx_len = kc_ref.shape[1]
    nsub = blk // chunk

    def ctx_state(weights):
        kw = (kc_ref[0].astype(F32) * weights).T.astype(BF16)
        r_scr[...] = jnp.dot(kw, vc_ref[0], preferred_element_type=F32)

    jc = lax.broadcasted_iota(jnp.int32, (ctx_len, 1), 0).astype(F32)

    @pl.when(step == 0)
    def _():
        ctx_state(jnp.exp((ctx_len - 1.0 - jc) * lg))

    @pl.when(step == nblk)
    def _():
        ctx_state(jnp.exp(jc * lg))

    def pos(shape, dim):
        i = lax.broadcasted_iota(jnp.int32, shape, dim)
        return jnp.where(bwd, chunk - 1 - i, i).astype(F32)

    diff = pos((chunk, chunk), 0) - pos((chunk, chunk), 1)
    intra = jnp.where(diff >= 0, jnp.exp(jnp.maximum(diff, 0.0) * lg), 0.0)
    pcol = pos((chunk, 1), 0)
    q_dec = jnp.exp((pcol + 1.0) * lg)
    k_dec = jnp.exp((chunk - 1.0 - pcol) * lg)
    chunk_dec = jnp.exp(chunk * lg)

    blk_idx = jnp.where(bwd, 2 * nblk - 1 - step, step)
    r = r_scr[...]
    for j in range(nsub):
        row0 = pl.multiple_of(jnp.where(bwd, nsub - 1 - j, j) * chunk, chunk)
        rows = pl.ds(row0, chunk)
        q, k, v = q_ref[0, rows, :], k_ref[0, rows, :], v_ref[0, rows, :]
        att = lax.dot_general(q, k, (((1,), (1,)), ((), ())), preferred_element_type=F32) * intra
        o = (jnp.dot(att.astype(BF16), v, preferred_element_type=F32)
             + jnp.dot(q, r.astype(BF16), preferred_element_type=F32) * q_dec)
        kd = (k.astype(F32) * k_dec).T.astype(BF16)
        r = r * chunk_dec + jnp.dot(kd, v, preferred_element_type=F32)
        grows = pl.ds(pl.multiple_of(blk_idx * blk + row0, chunk), chunk)
        o_scr[grows, :] = jnp.where(bwd, o_scr[grows, :] + o, o)
    r_scr[...] = r

    @pl.when(bwd)
    def _():
        y = o_scr[pl.ds(pl.multiple_of(blk_idx * blk, blk), blk), :]
        mu = jnp.mean(y, axis=-1, keepdims=True)
        yc = y - mu
        var = jnp.mean(yc * yc, axis=-1, keepdims=True)
        yn = yc * lax.rsqrt(var + EPS) * gn_ref[...]
        g = g_ref[0].astype(F32)
        z_ref[0] = (g * _sigmoid(g) * yn).astype(z_ref.dtype)


def _ret_scan(log_gamma, q, k, v, g, kc, vc, gn_w):
    bsz, s, nq = q.shape
    heads = RET_HEADS
    dk, dv = nq // heads, v.shape[-1] // heads
    chunk = min(RET_CHUNK, s)
    blk = min(RET_BLOCK, s)
    assert s % blk == 0 and blk % chunk == 0
    n = s // blk
    ctx_len = kc.shape[1]

    def cidx(st):
        return jnp.where(st < n, st, 2 * n - 1 - st)

    def gidx(st):
        return jnp.where(st < n, n - 1, 2 * n - 1 - st)

    grid_spec = pltpu.PrefetchScalarGridSpec(
        num_scalar_prefetch=1,
        grid=(bsz, heads, 2 * n),
        in_specs=[
            pl.BlockSpec((1, blk, dk), lambda b, h, st, lg: (b, cidx(st), h)),
            pl.BlockSpec((1, blk, dk), lambda b, h, st, lg: (b, cidx(st), h)),
            pl.BlockSpec((1, blk, dv), lambda b, h, st, lg: (b, cidx(st), h)),
            pl.BlockSpec((1, blk, dv), lambda b, h, st, lg: (b, gidx(st), h)),
            pl.BlockSpec((1, ctx_len, dk), lambda b, h, st, lg: (b, 0, h)),
            pl.BlockSpec((1, ctx_len, dv), lambda b, h, st, lg: (b, 0, h)),
            pl.BlockSpec((1, dv), lambda b, h, st, lg: (0, h)),
        ],
        out_specs=pl.BlockSpec((1, blk, dv), lambda b, h, st, lg: (b, gidx(st), h)),
        scratch_shapes=[pltpu.VMEM((dk, dv), F32), pltpu.VMEM((s, dv), F32)],
    )
    return pl.pallas_call(
        functools.partial(_ret_scan_kernel, nblk=n, blk=blk, chunk=chunk, heads=heads),
        grid_spec=grid_spec,
        out_shape=jax.ShapeDtypeStruct(v.shape, BF16),
        compiler_params=_cparams(("parallel", "parallel", "arbitrary")),
        name="ret_scan",
    )(log_gamma, q, k, v, g, kc, vc, gn_w)


def _trunk(x, c, ctx, c_ctx, ada_w, ada_b, norm_w, pool_w, pool_b, pool_scale,
           attn_w_qkv, attn_q_gain, attn_k_gain, attn_w_o,
           ret_w_in, ret_decay_logit, ret_gn_w, ret_w_out,
           ffn_w_up, ffn_conv_w, ffn_conv_b, ffn_w_down):
    bsz, s, d = x.shape
    depth = ada_w.shape[0]
    rows = jnp.concatenate([c, c_ctx[None, :], jnp.zeros((8 - bsz - 1, d), F32)], axis=0)
    mods = _ada_all(rows, ada_w, ada_b)

    def mod_vecs(i):
        m = mods[i].reshape(8, 6, d)
        lat = [m[:bsz, j][:, None, :] for j in range(6)]
        cx = [jnp.broadcast_to(m[bsz, j][None, None, :], (bsz, 1, d)) for j in range(6)]
        return lat, cx

    ctx_s = ctx
    for i in range(depth):
        kind = i % N_MIXERS
        j = i // N_MIXERS
        need_ctx_out = any(k % N_MIXERS != 0 for k in range(i + 1, depth))
        need_ctx_in = need_ctx_out or kind != 0
        (sh1, sc1, g1, sh2, sc2, g2), (csh1, csc1, cg1, csh2, csc2, cg2) = mod_vecs(i)
        gain1, gain2 = norm_w[i, 0][None, :], norm_w[i, 1][None, :]

        if kind == 0:
            pw = pool_w[j].astype(BF16)
            pb, ps = pool_b[j][None, :], pool_scale[j][None, :]
            x = _pool_layer(x, gain1, sh1, sc1, g1, pw, pb, ps)
            if need_ctx_out:
                ctx_s = _pool_layer(ctx_s, gain1, csh1, csc1, cg1, pw, pb, ps)
        elif kind == 1:
            hd = attn_q_gain.shape[-1]
            nqk = (ATTN_HEADS + ATTN_KV_HEADS) * hd
            perm = _rope_perm(hd)
            cols = (jnp.arange(nqk // hd)[:, None] * hd + perm[None, :]).reshape(-1)
            w_all = attn_w_qkv[j].astype(BF16)
            w = w_all[:, cols]
            wvt = w_all[:, nqk:].T
            wo = attn_w_o[j].astype(BF16)
            qg, kg = attn_q_gain[j][perm][None, :], attn_k_gain[j][perm][None, :]
            cos, sin = _rope_tables(s, hd)
            qc, kc, vct = _qkv_proj(ctx_s, gain1, csh1, csc1, w, wvt, qg, kg, None, None)
            ql, kl, vlt = _qkv_proj(x, gain1, sh1, sc1, w, wvt, qg, kg, cos, sin)
            keys = jnp.concatenate([kl, kc], axis=1)
            vals_t = jnp.concatenate([vlt, vct], axis=2)
            bounded = _logits_bounded(qg, kg, qg.shape[-1])
            x = _proj_res(x, _flash(bounded, ql, keys, vals_t), g1, wo)
            if need_ctx_out:
                ctx_s = _proj_res(ctx_s, _flash(bounded, qc, kc, vct), cg1, wo)
        else:
            w_in = ret_w_in[j].astype(BF16)
            w_out = ret_w_out[j].astype(BF16)
            heads = RET_HEADS
            nq = d
            nv = ret_gn_w.shape[-1]
            log_gamma = jax.nn.log_sigmoid(ret_decay_logit[j].astype(F32)).reshape(-1)
            kscale = (nq // heads) ** -0.5
            gn = ret_gn_w[j][None, :]
            assert not need_ctx_out
            kc, vc = _ret_proj(ctx_s, gain1, csh1, csc1, w_in[:, nq:2 * nq + nv],
                               (nq, nv), (BF16, BF16), kscale, 0)
            ql, kl, vl, gl = _ret_proj(x, gain1, sh1, sc1, w_in, (nq, nq, nv, nv),
                                       (BF16, BF16, BF16, BF16), kscale, 1)
            z = _ret_scan(log_gamma, ql, kl, vl, gl, kc, vc, gn)
            x = _proj_res(x, z, g1, w_out)

        wu, wd = ffn_w_up[i].astype(BF16), ffn_w_down[i].astype(BF16)
        cw, cb = ffn_conv_w[i], ffn_conv_b[i][None, :]
        x = _ffn_layer(x, gain2, sh2, sc2, g2, wu, cw, cb, wd)
        if need_ctx_out:
            ctx_s = _ffn_layer(ctx_s, gain2, csh2, csc2, cg2, wu, cw, cb, wd)
    return x


def kernel(x, c, ctx, c_ctx, ada_w, ada_b, norm_w, pool_w, pool_b, pool_scale, attn_w_qkv, attn_q_gain, attn_k_gain, attn_w_o, ret_w_in, ret_decay_logit, ret_gn_w, ret_w_out, ffn_w_up, ffn_conv_w, ffn_conv_b, ffn_w_down):
    return _trunk(x, c, ctx, c_ctx, ada_w, ada_b, norm_w, pool_w, pool_b, pool_scale,
                  attn_w_qkv, attn_q_gain, attn_k_gain, attn_w_o,
                  ret_w_in, ret_decay_logit, ret_gn_w, ret_w_out,
                  ffn_w_up, ffn_conv_w, ffn_conv_b, ffn_w_down)
```

```python
import functools

import jax
import jax.numpy as jnp
from jax import lax
from jax.experimental import pallas as pl
from jax.experimental.pallas import tpu as pltpu

F32 = jnp.float32
BF16 = jnp.bfloat16

EPS = 1e-6
LOG2E = 1.4426950408889634
SAFE_LOG2_LOGIT = 60.0
N_MIXERS = 3
POOL_WINDOWS = (2, 4, 8, 16)
ATTN_HEADS = 8
ATTN_KV_HEADS = 2
ATTN_GROUP = ATTN_HEADS // ATTN_KV_HEADS
GRID_W = 64
ROPE_THETA = 10000.0
RET_HEADS = 4
CONV_WIDTH = 3

HALO = 16
FLASH_KEY_TILE = 1024
RET_CHUNK = 256
RET_BLOCK = 1024
VMEM_LIMIT = 52 * 1024 * 1024


def _cparams(sem):
    return pltpu.CompilerParams(dimension_semantics=sem, vmem_limit_bytes=VMEM_LIMIT)


def _resident(shape):
    nd = len(shape)
    return pl.BlockSpec(shape, lambda *_: (0,) * nd, pipeline_mode=pl.Buffered(1))


def _row_tile(s, want):
    t = min(want, s)
    assert s % t == 0 and t % HALO == 0
    return t


def _sigmoid(x):
    return 1.0 / (1.0 + jnp.exp(-x))


def _norm_mod(x, gain, scale, shift):
    ms = jnp.mean(x * x, axis=-1, keepdims=True)
    return x * lax.rsqrt(ms + EPS) * (gain * (1.0 + scale)) + shift


def _halo_specs(tm, s, d):
    r = tm // HALO
    last = s // HALO - 1
    main = pl.BlockSpec((1, tm, d), lambda b, i: (b, i, 0))
    prev = pl.BlockSpec((1, HALO, d), lambda b, i: (b, jnp.maximum(i * r - 1, 0), 0))
    nxt = pl.BlockSpec((1, HALO, d), lambda b, i: (b, jnp.minimum((i + 1) * r, last), 0))
    return main, prev, nxt


def _vec_spec(d):
    return pl.BlockSpec((1, 1, d), lambda b, i: (b, 0, 0))


def _ada_kernel(c_ref, w_ref, b_ref, o_ref):
    c = c_ref[...]
    s = (c * _sigmoid(c)).astype(BF16)
    o_ref[0] = jnp.dot(s, w_ref[0].astype(BF16), preferred_element_type=F32) + b_ref[0]


def _ada_all(rows, ada_w, ada_b):
    depth, d, n = ada_w.shape
    tn = 1536
    assert n % tn == 0
    return pl.pallas_call(
        _ada_kernel,
        grid=(depth, n // tn),
        in_specs=[
            pl.BlockSpec(rows.shape, lambda l, j: (0, 0)),
            pl.BlockSpec((1, d, tn), lambda l, j: (l, 0, j)),
            pl.BlockSpec((1, 1, tn), lambda l, j: (l, 0, j)),
        ],
        out_specs=pl.BlockSpec((1, rows.shape[0], tn), lambda l, j: (l, 0, j)),
        out_shape=jax.ShapeDtypeStruct((depth, rows.shape[0], n), F32),
        compiler_params=_cparams(("parallel", "parallel")),
        name="ada_mod",
    )(rows, ada_w, ada_b.reshape(depth, 1, n))


def _pool_kernel(x_ref, xp_ref, xn_ref, gain_ref, sh_ref, sc_ref, gt_ref,
                 w_ref, b_ref, ps_ref, o_ref, *, tm, seq):
    i = pl.program_id(1)
    nt = pl.num_programs(1)
    gain, sh, sc, gt = gain_ref[...], sh_ref[0], sc_ref[0], gt_ref[0]
    x = x_ref[0]
    hm = _norm_mod(x, gain, sc, sh)
    hp = jnp.where(i > 0, _norm_mod(xp_ref[0], gain, sc, sh), 0.0)
    hn = jnp.where(i < nt - 1, _norm_mod(xn_ref[0], gain, sc, sh), 0.0)
    hext = jnp.concatenate([hp, hm, hn], axis=0)
    n = tm + 2 * HALO
    t = (i * tm + lax.broadcasted_iota(jnp.int32, (tm, 1), 0)).astype(F32)
    gw = x.shape[1] // len(POOL_WINDOWS)
    for g, win in enumerate(POOL_WINDOWS):
        sl = slice(g * gw, (g + 1) * gw)
        hg = hext[:, sl]
        acc = hg + pltpu.roll(hg, 1, 0)
        half = 1
        while 2 * half < win:
            acc = pltpu.roll(acc, half, 0) + pltpu.roll(acc, n - half, 0)
            half *= 2
        cnt = jnp.minimum(t + win // 2, float(seq)) - jnp.maximum(t - win // 2, 0.0)
        mean = acc[HALO:HALO + tm] * (1.0 / cnt)
        dlt = (mean - hm[:, sl]).astype(BF16)
        y = jnp.dot(dlt, w_ref[g], preferred_element_type=F32)
        y = (y + b_ref[:, sl]) * ps_ref[:, sl]
        o_ref[0, :, sl] = x[:, sl] + gt[:, sl] * y


def _pool_layer(x, gain, sh, sc, gt, w, b, ps):
    bsz, s, d = x.shape
    tm = _row_tile(s, 512)
    main, prev, nxt = _halo_specs(tm, s, d)
    vec = _vec_spec(d)
    return pl.pallas_call(
        functools.partial(_pool_kernel, tm=tm, seq=s),
        grid=(bsz, s // tm),
        in_specs=[main, prev, nxt, _resident((1, d)), vec, vec, vec,
                  _resident(w.shape), _resident((1, d)), _resident((1, d))],
        out_specs=main,
        out_shape=jax.ShapeDtypeStruct(x.shape, F32),
        compiler_params=_cparams(("parallel", "parallel")),
        name="pool_layer",
    )(x, x, x, gain, sh, sc, gt, w, b, ps)


def _ffn_kernel(x_ref, xp_ref, xn_ref, gain_ref, sh_ref, sc_ref, gt_ref,
                wup_ref, cw_ref, cb_ref, wdn_ref, o_ref, h_scr, act_scr, *, tm, dff, cw):
    i = pl.program_id(1)
    nt = pl.num_programs(1)
    gain, sh, sc = gain_ref[...], sh_ref[0], sc_ref[0]
    x = x_ref[0]
    n = tm + 2 * HALO
    h_scr[0:HALO] = jnp.where(i > 0, _norm_mod(xp_ref[0], gain, sc, sh), 0.0).astype(BF16)
    h_scr[HALO:HALO + tm] = _norm_mod(x, gain, sc, sh).astype(BF16)
    h_scr[HALO + tm:n] = jnp.where(i < nt - 1, _norm_mod(xn_ref[0], gain, sc, sh), 0.0).astype(BF16)
    def conv(u, col):
        um = pltpu.roll(u, 1, 0)[HALO:HALO + tm]
        up = pltpu.roll(u, n - 1, 0)[HALO:HALO + tm]
        return (um * cw_ref[0:1, col] + u[HALO:HALO + tm] * cw_ref[1:2, col]
                + up * cw_ref[2:3, col] + cb_ref[:, col])

    for j in range(dff // cw):
        ca = slice(j * cw, (j + 1) * cw)
        cv = slice(dff + j * cw, dff + (j + 1) * cw)
        h = h_scr[...]
        ua = jnp.dot(h, wup_ref[:, ca], preferred_element_type=F32)
        uv = jnp.dot(h, wup_ref[:, cv], preferred_element_type=F32)
        a = conv(ua, ca)
        v = conv(uv, cv)
        act_scr[:, ca] = (a * _sigmoid(a) * v).astype(BF16)
    y = jnp.dot(act_scr[...], wdn_ref[...], preferred_element_type=F32)
    o_ref[0] = x + gt_ref[0] * y


def _ffn_layer(x, gain, sh, sc, gt, w_up, conv_w, conv_b, w_dn):
    bsz, s, d = x.shape
    dff = w_dn.shape[0]
    tm = _row_tile(s, 1024)
    cw = 256
    assert dff % cw == 0
    main, prev, nxt = _halo_specs(tm, s, d)
    vec = _vec_spec(d)
    return pl.pallas_call(
        functools.partial(_ffn_kernel, tm=tm, dff=dff, cw=cw),
        grid=(bsz, s // tm),
        in_specs=[main, prev, nxt, _resident((1, d)), vec, vec, vec,
                  _resident(w_up.shape), _resident(conv_w.shape), _resident(conv_b.shape),
                  _resident(w_dn.shape)],
        out_specs=main,
        out_shape=jax.ShapeDtypeStruct(x.shape, F32),
        scratch_shapes=[pltpu.VMEM((tm + 2 * HALO, d), BF16), pltpu.VMEM((tm, dff), BF16)],
        compiler_params=_cparams(("parallel", "parallel")),
        name="conv_ffn",
    )(x, x, x, gain, sh, sc, gt, w_up, conv_w, conv_b, w_dn)


def _proj_res_kernel(x_ref, a_ref, gt_ref, w_ref, o_ref):
    y = jnp.dot(a_ref[0], w_ref[...], preferred_element_type=F32)
    o_ref[0] = x_ref[0] + gt_ref[0] * y


def _proj_res(x, a, gt, w):
    bsz, s, d = x.shape
    k = a.shape[-1]
    tm = _row_tile(s, 512)
    return pl.pallas_call(
        _proj_res_kernel,
        grid=(bsz, s // tm),
        in_specs=[pl.BlockSpec((1, tm, d), lambda b, i: (b, i, 0)),
                  pl.BlockSpec((1, tm, k), lambda b, i: (b, i, 0)),
                  _vec_spec(d), _resident(w.shape)],
        out_specs=pl.BlockSpec((1, tm, d), lambda b, i: (b, i, 0)),
        out_shape=jax.ShapeDtypeStruct(x.shape, F32),
        compiler_params=_cparams(("parallel", "parallel")),
        name="proj_residual",
    )(x, a, gt, w)


def _head_norm(xh, gain):
    ms = jnp.mean(xh * xh, axis=-1, keepdims=True)
    return xh * lax.rsqrt(ms + EPS) * gain


def _rope_perm(hd):
    q = hd // 4
    return jnp.concatenate([jnp.arange(0, q), jnp.arange(2 * q, 3 * q),
                            jnp.arange(q, 2 * q), jnp.arange(3 * q, 4 * q)])


def _qkv_kernel(*refs, nq, nkv, hd, rope):
    if rope:
        (x_ref, gain_ref, sh_ref, sc_ref, wqv_ref, wk_ref, qg_ref, kg_ref,
         cos_ref, sin_ref, cost_ref, sint_ref, qt_ref, k_ref, vt_ref) = refs
    else:
        (x_ref, gain_ref, sh_ref, sc_ref, wqv_ref, wk_ref, qg_ref, kg_ref,
         qt_ref, k_ref, vt_ref) = refs
    h = _norm_mod(x_ref[0], gain_ref[...], sc_ref[0], sh_ref[0]).astype(BF16)
    half = hd // 2
    uvt = lax.dot_general(wqv_ref[...], h, (((1,), (1,)), ((), ())),
                          preferred_element_type=F32)
    for hh in range(nq):
        ut = uvt[hh * hd:(hh + 1) * hd]
        ms = jnp.mean(ut * ut, axis=0, keepdims=True)
        y = ut * lax.rsqrt(ms + EPS) * qg_ref[...]
        if rope:
            x1, x2 = y[:half], y[half:]
            c, s = cost_ref[...], sint_ref[...]
            y = jnp.concatenate([x1 * c - x2 * s, x1 * s + x2 * c], axis=0)
        qt_ref[0, hh * hd:(hh + 1) * hd, :] = y.astype(BF16)
    vt_ref[0] = uvt[nq * hd:].astype(BF16)
    u2 = jnp.dot(h, wk_ref[...], preferred_element_type=F32)
    for hh in range(nkv):
        u = _head_norm(u2[:, hh * hd:(hh + 1) * hd], kg_ref[...])
        if rope:
            u = u * cos_ref[...] + pltpu.roll(u, half, 1) * sin_ref[...]
        k_ref[0, :, hh * hd:(hh + 1) * hd] = u.astype(BF16)


def _qkv_proj(x, gain, sh, sc, wqv_t, wk, qg_t, kg, tables):
    bsz, s, d = x.shape
    hd = kg.shape[-1]
    nq, nkv = ATTN_HEADS, ATTN_KV_HEADS
    tm = _row_tile(s, 512)
    rope = tables is not None
    vec = _vec_spec(d)
    row = lambda width: pl.BlockSpec((1, tm, width), lambda b, i: (b, i, 0))
    col = lambda height: pl.BlockSpec((1, height, tm), lambda b, i: (b, 0, i))
    qg_full = jnp.broadcast_to(qg_t, (hd, tm))
    in_specs = [row(d), _resident((1, d)), vec, vec, _resident(wqv_t.shape), _resident(wk.shape),
                _resident((hd, tm)), _resident((1, hd))]
    args = [x, gain, sh, sc, wqv_t, wk, qg_full, kg]
    if rope:
        cos, sin, cos_t, sin_t = tables
        tab = pl.BlockSpec((tm, hd), lambda b, i: (i, 0))
        tab_t = pl.BlockSpec((hd // 2, tm), lambda b, i: (0, i))
        in_specs += [tab, tab, tab_t, tab_t]
        args += [cos, sin, cos_t, sin_t]
    return pl.pallas_call(
        functools.partial(_qkv_kernel, nq=nq, nkv=nkv, hd=hd, rope=rope),
        grid=(bsz, s // tm),
        in_specs=in_specs,
        out_specs=[col(nq * hd), row(nkv * hd), col(nkv * hd)],
        out_shape=[jax.ShapeDtypeStruct((bsz, nq * hd, s), BF16),
                   jax.ShapeDtypeStruct((bsz, s, nkv * hd), BF16),
                   jax.ShapeDtypeStruct((bsz, nkv * hd, s), BF16)],
        compiler_params=_cparams(("parallel", "parallel")),
        name="qkv_rope" if rope else "qkv_ctx",
    )(*args)


def _flash_kernel(flag_ref, *refs, tq, tk, hd, grp, n):
    if n:
        (qt_ref, kc_ref, vct_ref, k_ref, vt_ref, o_ref,
         m_scr, l_scr, acc_scr, pc, p0, p1) = refs
    else:
        qt_ref, kc_ref, vct_ref, o_ref, m_scr, l_scr, acc_scr, pc = refs
    qt = jnp.concatenate([qt_ref[0, g * hd:(g + 1) * hd, :] for g in range(grp)], axis=1)
    l_scr[...] = jnp.zeros(l_scr.shape, F32)
    acc_scr[...] = jnp.zeros(acc_scr.shape, F32)

    def key_slice(t):
        return pl.ds(t * tk, tk) if isinstance(t, int) else pl.ds(pl.multiple_of(t * tk, tk), tk)

    def tile(t):
        if t is None:
            return kc_ref[0], vct_ref[0]
        return k_ref[0, key_slice(t), :], vt_ref[0, :, key_slice(t)]

    def produce(t, p_scr):
        p = jnp.exp2(jnp.dot(tile(t)[0], qt, preferred_element_type=F32))
        l_scr[...] += jnp.sum(p, axis=0, keepdims=True)
        p_scr[...] = p.astype(BF16)

    def consume(t, p_scr):
        acc_scr[...] += jnp.dot(tile(t)[1], p_scr[...], preferred_element_type=F32)

    @pl.when(flag_ref[0] == 1)
    def _():
        produce(None, pc)
        if n == 0:
            consume(None, pc)
            return
        produce(0, p0)
        consume(None, pc)

        def body(u, carry):
            produce(2 * u + 1, p1)
            consume(2 * u, p0)
            produce(2 * u + 2, p0)
            consume(2 * u + 1, p1)
            return carry

        pairs = (n - 1) // 2
        if pairs:
            lax.fori_loop(0, pairs, body, 0)
        if n % 2 == 0:
            produce(n - 1, p1)
            consume(n - 2, p0)
            consume(n - 1, p1)
        else:
            consume(n - 1, p0)

    @pl.when(flag_ref[0] != 1)
    def _():
        m_scr[...] = jnp.full(m_scr.shape, -jnp.inf, F32)

        def step(t):
            kt, vt = tile(t)
            s = jnp.dot(kt, qt, preferred_element_type=F32)
            m_old = m_scr[...]
            m_new = jnp.maximum(m_old, jnp.max(s, axis=0, keepdims=True))
            alpha = jnp.exp2(m_old - m_new)
            p = jnp.exp2(s - m_new)
            l_scr[...] = alpha * l_scr[...] + jnp.sum(p, axis=0, keepdims=True)
            acc_scr[...] = alpha * acc_scr[...] + jnp.dot(vt, p.astype(BF16),
                                                          preferred_element_type=F32)
            m_scr[...] = m_new

        def body(t, carry):
            step(t)
            return carry

        step(None)
        if n:
            lax.fori_loop(0, n, body, 0)

    o = acc_scr[...] / l_scr[...]
    for g in range(grp):
        o_ref[0, :, g * hd:(g + 1) * hd] = o[:, g * tq:(g + 1) * tq].T.astype(BF16)


def _flash(bounded, qt, kc, vct, k=None, vt=None):
    bsz, _, sq = qt.shape
    hd = kc.shape[2] // ATTN_KV_HEADS
    ctx_len = kc.shape[1]
    grp = ATTN_GROUP
    tq = min(256, sq)
    assert sq % tq == 0
    width = grp * tq
    in_specs = [pl.BlockSpec((1, grp * hd, tq), lambda b, kh, i, f: (b, kh, i)),
                pl.BlockSpec((1, ctx_len, hd), lambda b, kh, i, f: (b, 0, kh)),
                pl.BlockSpec((1, hd, ctx_len), lambda b, kh, i, f: (b, kh, 0))]
    args = [qt, kc, vct]
    scratch = [pltpu.VMEM((1, width), F32), pltpu.VMEM((1, width), F32),
               pltpu.VMEM((hd, width), F32), pltpu.VMEM((ctx_len, width), BF16)]
    tk, n = 0, 0
    if k is not None:
        skv = k.shape[1]
        tk = min(FLASH_KEY_TILE, skv)
        assert skv % tk == 0
        n = skv // tk
        in_specs += [pl.BlockSpec((1, skv, hd), lambda b, kh, i, f: (b, 0, kh)),
                     pl.BlockSpec((1, hd, skv), lambda b, kh, i, f: (b, kh, 0))]
        args += [k, vt]
        scratch += [pltpu.VMEM((tk, width), BF16), pltpu.VMEM((tk, width), BF16)]
    grid_spec = pltpu.PrefetchScalarGridSpec(
        num_scalar_prefetch=1,
        grid=(bsz, ATTN_KV_HEADS, sq // tq),
        in_specs=in_specs,
        out_specs=pl.BlockSpec((1, tq, grp * hd), lambda b, kh, i, f: (b, i, kh)),
        scratch_shapes=scratch,
    )
    return pl.pallas_call(
        functools.partial(_flash_kernel, tq=tq, tk=tk, hd=hd, grp=grp, n=n),
        grid_spec=grid_spec,
        out_shape=jax.ShapeDtypeStruct((bsz, sq, ATTN_HEADS * hd), BF16),
        compiler_params=_cparams(("parallel", "parallel", "arbitrary")),
        name="flash_gqa",
    )(bounded, *args)


def _logits_bounded(qg, kg, hd):
    bound = 1.02 * LOG2E * hd ** 0.5 * jnp.max(jnp.abs(qg)) * jnp.max(jnp.abs(kg))
    return (bound <= SAFE_LOG2_LOGIT).astype(jnp.int32).reshape(1)


def _rope_tables(s, hd):
    rows = s // GRID_W
    row = jnp.broadcast_to(jnp.arange(rows)[:, None], (rows, GRID_W)).reshape(-1).astype(F32)
    col = jnp.broadcast_to(jnp.arange(GRID_W)[None, :], (rows, GRID_W)).reshape(-1).astype(F32)
    axis_dim = hd // 2
    inv = ROPE_THETA ** (-jnp.arange(0, axis_dim, 2, dtype=F32) / axis_dim)
    ang = jnp.concatenate([row[:, None] * inv, col[:, None] * inv], axis=-1)
    c, sn = jnp.cos(ang), jnp.sin(ang)
    return (jnp.concatenate([c, c], axis=-1), jnp.concatenate([-sn, sn], axis=-1), c.T, sn.T)


def _ret_proj_kernel(x_ref, gain_ref, sh_ref, sc_ref, w_ref, *o_refs, kscale, k_index):
    h = _norm_mod(x_ref[0], gain_ref[...], sc_ref[0], sh_ref[0]).astype(BF16)
    off = 0
    for idx, o_ref in enumerate(o_refs):
        width = o_ref.shape[-1]
        for c0 in range(0, width, 1024):
            c1 = min(c0 + 1024, width)
            u = jnp.dot(h, w_ref[:, off + c0:off + c1], preferred_element_type=F32)
            if idx == k_index:
                u = u * kscale
            o_ref[0, :, c0:c1] = u.astype(o_ref.dtype)
        off += width


def _ret_proj(x, gain, sh, sc, w, widths, dtypes, kscale, k_index):
    bsz, s, d = x.shape
    tm = _row_tile(s, 512)
    vec = _vec_spec(d)
    row = lambda width: pl.BlockSpec((1, tm, width), lambda b, i: (b, i, 0))
    return pl.pallas_call(
        functools.partial(_ret_proj_kernel, kscale=kscale, k_index=k_index),
        grid=(bsz, s // tm),
        in_specs=[row(d), _resident((1, d)), vec, vec, _resident(w.shape)],
        out_specs=[row(wd) for wd in widths],
        out_shape=[jax.ShapeDtypeStruct((bsz, s, wd), dt) for wd, dt in zip(widths, dtypes)],
        compiler_params=_cparams(("parallel", "parallel")),
        name="ret_proj",
    )(x, gain, sh, sc, w)


def _ret_scan_kernel(lg_ref, q_ref, k_ref, v_ref, g_ref, kc_ref, vc_ref, gn_ref, z_ref,
                     r_scr, o_scr, *, nblk, blk, chunk, heads):
    hh = pl.program_id(1)
    step = pl.program_id(2)
    bwd = step >= nblk
    lg = jnp.zeros((1, 1), F32) + jnp.where(bwd, lg_ref[heads + hh], lg_ref[hh])
    ctx_len = kc_ref.shape[1]
    nsub = blk // chunk

    def ctx_state(weights):
        kw = (kc_ref[0].astype(F32) * weights).T.astype(BF16)
        r_scr[...] = jnp.dot(kw, vc_ref[0], preferred_element_type=F32)

    jc = lax.broadcasted_iota(jnp.int32, (ctx_len, 1), 0).astype(F32)

    @pl.when(step == 0)
    def _():
        ctx_state(jnp.exp((ctx_len - 1.0 - jc) * lg))

    @pl.when(step == nblk)
    def _():
        ctx_state(jnp.exp(jc * lg))

    def pos(shape, dim):
        i = lax.broadcasted_iota(jnp.int32, shape, dim)
        return jnp.where(bwd, chunk - 1 - i, i).astype(F32)

    diff = pos((chunk, chunk), 0) - pos((chunk, chunk), 1)
    intra = jnp.where(diff >= 0, jnp.exp(jnp.maximum(diff, 0.0) * lg), 0.0)
    pcol = pos((chunk, 1), 0)
    q_dec = jnp.exp((pcol + 1.0) * lg)
    k_dec = jnp.exp((chunk - 1.0 - pcol) * lg)
    chunk_dec = jnp.exp(chunk * lg)

    blk_idx = jnp.where(bwd, 2 * nblk - 1 - step, step)
    r = r_scr[...]
    for j in range(nsub):
        row0 = pl.multiple_of(jnp.where(bwd, nsub - 1 - j, j) * chunk, chunk)
        rows = pl.ds(row0, chunk)
        q, k, v = q_ref[0, rows, :], k_ref[0, rows, :], v_ref[0, rows, :]
        att = lax.dot_general(q, k, (((1,), (1,)), ((), ())), preferred_element_type=F32) * intra
        o = (jnp.dot(att.astype(BF16), v, preferred_element_type=F32)
             + jnp.dot(q, r.astype(BF16), preferred_element_type=F32) * q_dec)
        kd = (k.astype(F32) * k_dec).T.astype(BF16)
        r = r * chunk_dec + jnp.dot(kd, v, preferred_element_type=F32)
        grows = pl.ds(pl.multiple_of(blk_idx * blk + row0, chunk), chunk)
        o_scr[grows, :] = jnp.where(bwd, o_scr[grows, :] + o, o)
    r_scr[...] = r

    @pl.when(bwd)
    def _():
        y = o_scr[pl.ds(pl.multiple_of(blk_idx * blk, blk), blk), :]
        mu = jnp.mean(y, axis=-1, keepdims=True)
        yc = y - mu
        var = jnp.mean(yc * yc, axis=-1, keepdims=True)
        yn = yc * lax.rsqrt(var + EPS) * gn_ref[...]
        g = g_ref[0].astype(F32)
        z_ref[0] = (g * _sigmoid(g) * yn).astype(z_ref.dtype)


def _ret_scan(log_gamma, q, k, v, g, kc, vc, gn_w):
    bsz, s, nq = q.shape
    heads = RET_HEADS
    dk, dv = nq // heads, v.shape[-1] // heads
    chunk = min(RET_CHUNK, s)
    blk = min(RET_BLOCK, s)
    assert s % blk == 0 and blk % chunk == 0
    n = s // blk
    ctx_len = kc.shape[1]

    def cidx(st):
        return jnp.where(st < n, st, 2 * n - 1 - st)

    def gidx(st):
        return jnp.where(st < n, n - 1, 2 * n - 1 - st)

    grid_spec = pltpu.PrefetchScalarGridSpec(
        num_scalar_prefetch=1,
        grid=(bsz, heads, 2 * n),
        in_specs=[
            pl.BlockSpec((1, blk, dk), lambda b, h, st, lg: (b, cidx(st), h)),
            pl.BlockSpec((1, blk, dk), lambda b, h, st, lg: (b, cidx(st), h)),
            pl.BlockSpec((1, blk, dv), lambda b, h, st, lg: (b, cidx(st), h)),
            pl.BlockSpec((1, blk, dv), lambda b, h, st, lg: (b, gidx(st), h)),
            pl.BlockSpec((1, ctx_len, dk), lambda b, h, st, lg: (b, 0, h)),
            pl.BlockSpec((1, ctx_len, dv), lambda b, h, st, lg: (b, 0, h)),
            pl.BlockSpec((1, dv), lambda b, h, st, lg: (0, h)),
        ],
        out_specs=pl.BlockSpec((1, blk, dv), lambda b, h, st, lg: (b, gidx(st), h)),
        scratch_shapes=[pltpu.VMEM((dk, dv), F32), pltpu.VMEM((s, dv), F32)],
    )
    return pl.pallas_call(
        functools.partial(_ret_scan_kernel, nblk=n, blk=blk, chunk=chunk, heads=heads),
        grid_spec=grid_spec,
        out_shape=jax.ShapeDtypeStruct(v.shape, BF16),
        compiler_params=_cparams(("parallel", "parallel", "arbitrary")),
        name="ret_scan",
    )(log_gamma, q, k, v, g, kc, vc, gn_w)


def _trunk(x, c, ctx, c_ctx, ada_w, ada_b, norm_w, pool_w, pool_b, pool_scale,
           attn_w_qkv, attn_q_gain, attn_k_gain, attn_w_o,
           ret_w_in, ret_decay_logit, ret_gn_w, ret_w_out,
           ffn_w_up, ffn_conv_w, ffn_conv_b, ffn_w_down):
    bsz, s, d = x.shape
    depth = ada_w.shape[0]
    rows = jnp.concatenate([c, c_ctx[None, :], jnp.zeros((8 - bsz - 1, d), F32)], axis=0)
    mods = _ada_all(rows, ada_w, ada_b)

    def mod_vecs(i):
        m = mods[i].reshape(8, 6, d)
        lat = [m[:bsz, j][:, None, :] for j in range(6)]
        cx = [jnp.broadcast_to(m[bsz, j][None, None, :], (bsz, 1, d)) for j in range(6)]
        return lat, cx

    ctx_s = ctx
    for i in range(depth):
        kind = i % N_MIXERS
        j = i // N_MIXERS
        need_ctx_out = any(k % N_MIXERS != 0 for k in range(i + 1, depth))
        need_ctx_in = need_ctx_out or kind != 0
        (sh1, sc1, g1, sh2, sc2, g2), (csh1, csc1, cg1, csh2, csc2, cg2) = mod_vecs(i)
        gain1, gain2 = norm_w[i, 0][None, :], norm_w[i, 1][None, :]

        if kind == 0:
            pw = pool_w[j].astype(BF16)
            pb, ps = pool_b[j][None, :], pool_scale[j][None, :]
            x = _pool_layer(x, gain1, sh1, sc1, g1, pw, pb, ps)
            if need_ctx_out:
                ctx_s = _pool_layer(ctx_s, gain1, csh1, csc1, cg1, pw, pb, ps)
        elif kind == 1:
            hd = attn_q_gain.shape[-1]
            nq_cols = ATTN_HEADS * hd
            nqk = (ATTN_HEADS + ATTN_KV_HEADS) * hd
            perm = _rope_perm(hd)
            cols = (jnp.arange(nqk // hd)[:, None] * hd + perm[None, :]).reshape(-1)
            w_all = attn_w_qkv[j].astype(BF16)
            w_qk = w_all[:, cols]
            wqv_t = jnp.concatenate([w_qk[:, :nq_cols], w_all[:, nqk:]], axis=1).T
            wk = w_qk[:, nq_cols:]
            wo = attn_w_o[j].astype(BF16)
            qg_t = (attn_q_gain[j][perm] * (hd ** -0.5 * LOG2E))[:, None]
            kg = attn_k_gain[j][perm][None, :]
            tables = _rope_tables(s, hd)
            qct, kc, vct = _qkv_proj(ctx_s, gain1, csh1, csc1, wqv_t, wk, qg_t, kg, None)
            qlt, kl, vlt = _qkv_proj(x, gain1, sh1, sc1, wqv_t, wk, qg_t, kg, tables)
            bounded = _logits_bounded(attn_q_gain[j], attn_k_gain[j], hd)
            x = _proj_res(x, _flash(bounded, qlt, kc, vct, kl, vlt), g1, wo)
            if need_ctx_out:
                ctx_s = _proj_res(ctx_s, _flash(bounded, qct, kc, vct), cg1, wo)
        else:
            w_in = ret_w_in[j].astype(BF16)
            w_out = ret_w_out[j].astype(BF16)
            heads = RET_HEADS
            nq = d
            nv = ret_gn_w.shape[-1]
            log_gamma = jax.nn.log_sigmoid(ret_decay_logit[j].astype(F32)).reshape(-1)
            kscale = (nq // heads) ** -0.5
            gn = ret_gn_w[j][None, :]
            assert not need_ctx_out
            kc, vc = _ret_proj(ctx_s, gain1, csh1, csc1, w_in[:, nq:2 * nq + nv],
                               (nq, nv), (BF16, BF16), kscale, 0)
            ql, kl, vl, gl = _ret_proj(x, gain1, sh1, sc1, w_in, (nq, nq, nv, nv),
                                       (BF16, BF16, BF16, BF16), kscale, 1)
            z = _ret_scan(log_gamma, ql, kl, vl, gl, kc, vc, gn)
            x = _proj_res(x, z, g1, w_out)

        wu, wd = ffn_w_up[i].astype(BF16), ffn_w_down[i].astype(BF16)
        cw, cb = ffn_conv_w[i], ffn_conv_b[i][None, :]
        x = _ffn_layer(x, gain2, sh2, sc2, g2, wu, cw, cb, wd)
        if need_ctx_out:
            ctx_s = _ffn_layer(ctx_s, gain2, csh2, csc2, cg2, wu, cw, cb, wd)
    return x


def kernel(x, c, ctx, c_ctx, ada_w, ada_b, norm_w, pool_w, pool_b, pool_scale, attn_w_qkv, attn_q_gain, attn_k_gain, attn_w_o, ret_w_in, ret_decay_logit, ret_gn_w, ret_w_out, ffn_w_up, ffn_conv_w, ffn_conv_b, ffn_w_down):
    return _trunk(x, c, ctx, c_ctx, ada_w, ada_b, norm_w, pool_w, pool_b, pool_scale,
                  attn_w_qkv, attn_q_gain, attn_k_gain, attn_w_o,
                  ret_w_in, ret_decay_logit, ret_gn_w, ret_w_out,
                  ffn_w_up, ffn_conv_w, ffn_conv_b, ffn_w_down)
```
